```python
import math
import jax
import jax.numpy as jnp
from jax import lax
import numpy as np

D_MODEL = 2048
BATCH = 4
SEQ = 2048
DEPTH = 4
DEC_BATCH = 128
DEC_SEQ = 4
PAST_LEN = 8192
PAGE_SIZE = 128

MLA_HEADS = 8
MLA_D_NOPE = 64
MLA_D_ROPE = 32
MLA_D_V = 64
MLA_D_C = 128
MLA_D_Q = 384
MLA_ROW = MLA_D_C + MLA_D_ROPE
ROPE_BASE = 10000.0
DIFF_HEADS = 4
DIFF_D = 64
DIFF_D_V = 2 * DIFF_D
NSA_HEADS = 8
NSA_D = 64
NSA_BLOCK = 64
NSA_TOP_N = 16
NSA_WINDOW = 512
N_BRANCH = 3
BRANCH_W = 512
D_FF = 5632
N_EXPERTS = 8
TOP_K = 2
D_FF_EXPERT = 1408
N_DENSE = (DEPTH + 1) // 2
N_MOE = DEPTH // 2
EPS = 1e-6
Q_BLOCK = 128

IN_SIZES = (MLA_D_Q, MLA_D_C, MLA_D_ROPE,
            DIFF_HEADS * 2 * DIFF_D, 2 * DIFF_D, DIFF_D_V,
            NSA_HEADS * NSA_D, 6 * NSA_D, 3 * NSA_HEADS,
            N_BRANCH * D_MODEL)
N_IN = sum(IN_SIZES)

kernel_name = 'hybrid_mla_diff_nsa_decoder_step'


def rms_norm(x, g):
    xf = x.astype(jnp.float32)
    y = xf * lax.rsqrt(jnp.mean(xf * xf, axis=-1, keepdims=True) + EPS)
    return (y * g.astype(jnp.float32)).astype(x.dtype)


def alibi_slopes(n):
    return jnp.exp2(-8.0 * jnp.arange(1, n + 1, dtype=jnp.float32) / n)


def rope(x, pos):
    half = x.shape[-1] // 2
    freqs = ROPE_BASE ** (-jnp.arange(half, dtype=jnp.float32) / half)
    ang = pos.astype(jnp.float32)[:, None] * freqs
    ang = ang.reshape((ang.shape[0],) + (1,) * (x.ndim - 3) + (half,))
    cos = jnp.cos(ang).astype(x.dtype)
    sin = jnp.sin(ang).astype(x.dtype)
    x1, x2 = x[..., :half], x[..., half:]
    return jnp.concatenate([x1 * cos - x2 * sin, x1 * sin + x2 * cos], axis=-1)


def masked_softmax(s, mask):
    s = jnp.where(mask, s.astype(jnp.float32), -1e30)
    return jax.nn.softmax(s, axis=-1) * mask


def sweep(fn, q_pos, *qs):
    t = q_pos.shape[0]
    qb = min(Q_BLOCK, t)
    nb = t // qb
    if nb == 1:
        return fn(q_pos, *qs)
    blocks = tuple(jnp.moveaxis(q.reshape((q.shape[0], nb, qb) + q.shape[2:]), 1, 0) for q in qs)
    out = lax.map(lambda a: fn(a[0], *a[1]), (q_pos.reshape(nb, qb), blocks))
    out = jnp.moveaxis(out, 0, 1)
    return out.reshape((out.shape[0], t) + out.shape[3:])


def split_cols(p):
    outs, start = [], 0
    for w in IN_SIZES:
        outs.append(p[..., start:start + w])
        start += w
    return outs


def mla_attend(q_nope, q_rope, q_pos, kv_rows, w_uk, w_uv):
    ckv = kv_rows[..., :MLA_D_C]
    kr = kv_rows[..., MLA_D_C:]
    k_pos = jnp.arange(kv_rows.shape[1])
    scale = (MLA_D_NOPE + MLA_D_ROPE) ** -0.5
    q_lat = jnp.einsum('bthn,chn->bthc', q_nope, w_uk)

    def block(pos, ql, qr):
        s = jnp.einsum('bthc,bsc->bhts', ql, ckv) + jnp.einsum('bthr,bsr->bhts', qr, kr)
        p = masked_softmax(s.astype(jnp.float32) * scale, k_pos[None, :] <= pos[:, None])
        o_lat = jnp.einsum('bhts,bsc->bthc', p.astype(ckv.dtype), ckv)
        return jnp.einsum('bthc,chv->bthv', o_lat, w_uv)

    o = sweep(block, q_pos, q_lat, q_rope)
    return o.reshape(o.shape[0], o.shape[1], MLA_HEADS * MLA_D_V)


def diff_attend(q, q_pos, kv_rows, lam, lam_init, g_sub):
    b, l = kv_rows.shape[0], kv_rows.shape[1]
    k = kv_rows[:, :, 0].reshape(b, l, 2, DIFF_D)
    v = kv_rows[:, :, 1]
    k_pos = jnp.arange(l)
    slopes = alibi_slopes(DIFF_HEADS)[:, None, None]

    def block(pos, qb):
        s = jnp.einsum('bthid,bsid->bihts', qb, k).astype(jnp.float32) * DIFF_D ** -0.5
        dist = pos[:, None] - k_pos[None, :]
        s = s - slopes * dist.astype(jnp.float32)
        p = masked_softmax(s, dist >= 0)
        a = (p[:, 0] - lam * p[:, 1]).astype(v.dtype)
        return jnp.einsum('bhts,bsv->bthv', a, v)

    o = sweep(block, q_pos, q)
    o = rms_norm(o, g_sub) * (1.0 - lam_init)
    return o.reshape(o.shape[0], o.shape[1], DIFF_HEADS * DIFF_D_V)


def prompt_window(win_rows):
    padded = jnp.pad(win_rows, ((0, 0), (NSA_WINDOW, 0), (0, 0), (0, 0)))

    def fn(pos):
        qb = pos.shape[0]
        start = pos[0]
        blk = lax.dynamic_slice_in_dim(padded, start, NSA_WINDOW + qb, axis=1)
        w_pos = start - NSA_WINDOW + jnp.arange(NSA_WINDOW + qb)
        return blk[:, :, 0], blk[:, :, 1], w_pos

    return fn


def nsa_attend(q, gates, q_pos, kv_rows, window_fn, w_cmp, pos_cmp):
    b, l = kv_rows.shape[0], kv_rows.shape[1]
    n_blk = -(-l // NSA_BLOCK)
    rows = jnp.pad(kv_rows, ((0, 0), (0, n_blk * NSA_BLOCK - l), (0, 0), (0, 0)))
    rows = rows.reshape(b, n_blk, NSA_BLOCK, 4, NSA_D)
    blk = rows[:, :, :, :2] + jnp.swapaxes(pos_cmp, 0, 1)
    blk = jnp.transpose(blk, (0, 1, 3, 2, 4)).reshape(b, n_blk, 2, NSA_BLOCK * NSA_D)
    cmp = jnp.einsum('bnif,ifd->bnid', blk, w_cmp)
    k_cmp, v_cmp = cmp[:, :, 0], cmp[:, :, 1]
    k_sel, v_sel = rows[:, :, :, 2], rows[:, :, :, 3]
    blk_ids = jnp.arange(n_blk)
    blk_end = (blk_ids + 1) * NSA_BLOCK - 1
    slopes = alibi_slopes(NSA_HEADS)[:, None, None]
    scale = NSA_D ** -0.5
    n_sel = min(NSA_TOP_N, n_blk)
    b_idx = jnp.arange(b)[:, None, None]

    def block(pos, qb, gb):
        tq = pos.shape[0]
        dist_c = pos[:, None] - blk_end[None, :]
        s_c = jnp.einsum('bthd,bnd->bhtn', qb, k_cmp).astype(jnp.float32) * scale - slopes * dist_c.astype(jnp.float32)
        p_c = masked_softmax(s_c, dist_c >= 0)
        o_c = jnp.einsum('bhtn,bnd->bthd', p_c.astype(v_cmp.dtype), v_cmp)
        cur = pos // NSA_BLOCK
        forced = (blk_ids[None] == 0) | (blk_ids[None] == cur[:, None]) | (blk_ids[None] == cur[:, None] - 1)
        imp = jnp.where(forced, NSA_HEADS + 1.0, jnp.sum(p_c, axis=1))
        imp = jnp.where(blk_ids[None] > cur[:, None], -1.0, imp)
        _, idx = lax.top_k(imp, n_sel)
        ks = k_sel[b_idx, idx].reshape(b, tq, n_sel * NSA_BLOCK, NSA_D)
        vs = v_sel[b_idx, idx].reshape(b, tq, n_sel * NSA_BLOCK, NSA_D)
        s_pos = (idx[..., None] * NSA_BLOCK + jnp.arange(NSA_BLOCK)).reshape(b, tq, n_sel * NSA_BLOCK)
        dist_s = (pos[None, :, None] - s_pos)[:, None]
        s_s = jnp.einsum('bthd,btkd->bhtk', qb, ks).astype(jnp.float32) * scale - slopes * dist_s.astype(jnp.float32)
        p_s = masked_softmax(s_s, dist_s >= 0)
        o_s = jnp.einsum('bhtk,btkd->bthd', p_s.astype(vs.dtype), vs)
        kw, vw, w_pos = window_fn(pos)
        dist_w = pos[:, None] - w_pos[None, :]
        mask_w = (dist_w >= 0) & (dist_w < NSA_WINDOW) & (w_pos[None, :] >= 0)
        s_w = jnp.einsum('bthd,bsd->bhts', qb, kw).astype(jnp.float32) * scale - slopes * dist_w.astype(jnp.float32)
        p_w = masked_softmax(s_w, mask_w)
        o_w = jnp.einsum('bhts,bsd->bthd', p_w.astype(vw.dtype), vw)
        g = jax.nn.sigmoid(gb.astype(jnp.float32)).astype(qb.dtype)
        return g[..., 0:1] * o_c + g[..., 1:2] * o_s + g[..., 2:3] * o_w

    o = sweep(block, q_pos, q, gates)
    return o.reshape(o.shape[0], o.shape[1], NSA_HEADS * NSA_D)


def token_mixers(h, pos, past, lam_init, w_in, mla_g_q, mla_w_uq, mla_g_kv, mla_w_uk, mla_w_uv,
                 diff_lambda, diff_g_sub, nsa_w_cmp, nsa_pos_cmp, w_branch, w_out):
    b, t = h.shape[0], h.shape[1]
    cq, ckv, kr, dq, dk, dv, nq, nkv, ng, mg = split_cols(h @ w_in)
    q = (rms_norm(cq, mla_g_q) @ mla_w_uq).reshape(b, t, MLA_HEADS, MLA_D_NOPE + MLA_D_ROPE)
    q_nope = q[..., :MLA_D_NOPE]
    q_rope = rope(q[..., MLA_D_NOPE:], pos)
    mla_rows = jnp.concatenate([rms_norm(ckv, mla_g_kv), rope(kr, pos)], axis=-1)
    diff_rows = jnp.stack([dk, dv], axis=2)
    nkv = nkv.reshape(b, t, 6, NSA_D)
    nsa_rows = nkv[:, :, :4]
    win_rows = nkv[:, :, 4:]
    if past is None:
        mla_all, diff_all, nsa_all = mla_rows, diff_rows, nsa_rows
        window_fn = prompt_window(win_rows)
        win_state = win_rows[:, t - min(NSA_WINDOW, t):]
    else:
        past_mla, past_diff, past_nsa, win_buf, past_len = past
        mla_all = jnp.concatenate([past_mla, mla_rows], axis=1)
        diff_all = jnp.concatenate([past_diff, diff_rows], axis=1)
        nsa_all = jnp.concatenate([past_nsa, nsa_rows], axis=1)
        win_all = jnp.concatenate([win_buf, win_rows], axis=1)
        lb = win_buf.shape[1]
        w_pos = past_len - lb + jnp.arange(lb + t)
        window_fn = lambda p: (win_all[:, :, 0], win_all[:, :, 1], w_pos)
        n_all = win_all.shape[1]
        win_state = win_all[:, n_all - min(NSA_WINDOW, n_all):]
    o_mla = mla_attend(q_nope, q_rope, pos, mla_all, mla_w_uk, mla_w_uv)
    lam = (jnp.exp(jnp.sum((diff_lambda[0] * diff_lambda[1]).astype(jnp.float32)))
           - jnp.exp(jnp.sum((diff_lambda[2] * diff_lambda[3]).astype(jnp.float32))) + lam_init)
    o_diff = diff_attend(dq.reshape(b, t, DIFF_HEADS, 2, DIFF_D), pos, diff_all, lam, lam_init, diff_g_sub)
    o_nsa = nsa_attend(nq.reshape(b, t, NSA_HEADS, NSA_D), ng.reshape(b, t, NSA_HEADS, 3), pos,
                       nsa_all, window_fn, nsa_w_cmp, nsa_pos_cmp)
    o = jnp.stack([o_mla, o_diff, o_nsa], axis=2)
    branch = jnp.einsum('btnw,nwd->btnd', o, w_branch)
    g = jax.nn.sigmoid(mg.reshape(b, t, N_BRANCH, D_MODEL).astype(jnp.float32)).astype(h.dtype)
    out = jnp.sum(g * branch, axis=2) @ w_out
    return out, mla_rows, diff_rows, nsa_rows, win_state


def adaln(c, w, b):
    m = jax.nn.silu(c) @ w + b
    return jnp.split(m[:, None, :], 6, axis=-1)


def modulate(h, shift, scale):
    return h * (1.0 + scale) + shift


def swiglu(h, wg, wu, wd):
    return (jax.nn.silu(h @ wg) * (h @ wu)) @ wd


def moe_ffn(h, w_router, b_router, wg, wu, wd):
    logits = (h @ w_router + b_router).astype(jnp.float32)
    top_v, top_i = lax.top_k(logits, TOP_K)
    w = jax.nn.softmax(top_v, axis=-1)
    gates = jnp.sum(w[..., None] * jax.nn.one_hot(top_i, N_EXPERTS, dtype=jnp.float32), axis=-2).astype(h.dtype)
    a = jax.nn.silu(jnp.einsum('btd,edf->btef', h, wg)) * jnp.einsum('btd,edf->btef', h, wu)
    return jnp.einsum('btef,efd->btd', a * gates[..., None], wd)


def setup_inputs(seed: int = 0) -> dict:
    key = jax.random.key(seed)
    ks = iter(jax.random.split(key, 48))

    def nrm(shape, scale):
        return jax.random.normal(next(ks), shape, jnp.float32) * scale

    n_pages = PAST_LEN // PAGE_SIZE
    n_pool = (DEC_BATCH * n_pages * 5) // 4
    win_len = min(NSA_WINDOW, PAST_LEN)
    page_table = jax.random.permutation(next(ks), n_pool)[:DEC_BATCH * n_pages]
    page_table = page_table.reshape(DEC_BATCH, n_pages).astype(jnp.int32)
    d = D_MODEL
    return {
        'x_prompt': nrm((BATCH, SEQ, d), 1.0),
        'x_sample': nrm((DEC_BATCH, DEC_SEQ, d), 1.0),
        'cache_mla': nrm((DEPTH, n_pool, PAGE_SIZE, MLA_ROW), 1.0),
        'cache_diff': nrm((DEPTH, n_pool, PAGE_SIZE, 2, DIFF_D_V), 1.0),
        'cache_nsa': nrm((DEPTH, n_pool, PAGE_SIZE, 4, NSA_D), 1.0),
        'state_nsa_win': nrm((DEPTH, DEC_BATCH, win_len, 2, NSA_D), 1.0),
        'page_table': page_table,
        'c_prompt': nrm((BATCH, d), 1.0),
        'c_sample': nrm((DEC_BATCH, d), 1.0),
        'w_mod': nrm((DEPTH, d, 6 * d), 0.5 * d ** -0.5),
        'b_mod': nrm((DEPTH, 6 * d), 0.02),
        'g_attn': 1.0 + nrm((DEPTH, d), 0.01),
        'g_ffn': 1.0 + nrm((DEPTH, d), 0.01),
        'w_in': nrm((DEPTH, d, N_IN), d ** -0.5),
        'mla_g_q': 1.0 + nrm((DEPTH, MLA_D_Q), 0.01),
        'mla_w_uq': nrm((DEPTH, MLA_D_Q, MLA_HEADS * (MLA_D_NOPE + MLA_D_ROPE)), MLA_D_Q ** -0.5),
        'mla_g_kv': 1.0 + nrm((DEPTH, MLA_D_C), 0.01),
        'mla_w_uk': nrm((DEPTH, MLA_D_C, MLA_HEADS, MLA_D_NOPE), MLA_D_C ** -0.5),
        'mla_w_uv': nrm((DEPTH, MLA_D_C, MLA_HEADS, MLA_D_V), MLA_D_C ** -0.5),
        'diff_lambda': nrm((DEPTH, 4, DIFF_D), 0.1),
        'diff_g_sub': 1.0 + nrm((DEPTH, DIFF_D_V), 0.01),
        'nsa_w_cmp': nrm((DEPTH, 2, NSA_BLOCK * NSA_D, NSA_D), (NSA_BLOCK * NSA_D) ** -0.5),
        'nsa_pos_cmp': nrm((DEPTH, 2, NSA_BLOCK, NSA_D), 0.1),
        'w_branch': nrm((DEPTH, N_BRANCH, BRANCH_W, d), BRANCH_W ** -0.5),
        'w_out': nrm((DEPTH, d, d), d ** -0.5),
        'w_gate_dense': nrm((N_DENSE, d, D_FF), d ** -0.5),
        'w_up_dense': nrm((N_DENSE, d, D_FF), d ** -0.5),
        'w_down_dense': nrm((N_DENSE, D_FF, d), D_FF ** -0.5),
        'w_router': nrm((N_MOE, d, N_EXPERTS), d ** -0.5),
        'b_router': nrm((N_MOE, N_EXPERTS), 0.01),
        'w_gate_exp': nrm((N_MOE, N_EXPERTS, d, D_FF_EXPERT), d ** -0.5),
        'w_up_exp': nrm((N_MOE, N_EXPERTS, d, D_FF_EXPERT), d ** -0.5),
        'w_down_exp': nrm((N_MOE, N_EXPERTS, D_FF_EXPERT, d), D_FF_EXPERT ** -0.5),
        'g_final': 1.0 + nrm((d,), 0.01),
    }


def reference(x_prompt, x_sample, cache_mla, cache_diff, cache_nsa, state_nsa_win, page_table,
              c_prompt, c_sample, w_mod, b_mod, g_attn, g_ffn, w_in, mla_g_q, mla_w_uq, mla_g_kv,
              mla_w_uk, mla_w_uv, diff_lambda, diff_g_sub, nsa_w_cmp, nsa_pos_cmp, w_branch, w_out,
              w_gate_dense, w_up_dense, w_down_dense, w_router, b_router, w_gate_exp, w_up_exp,
              w_down_exp, g_final):
    n_pages = page_table.shape[1]
    past_len = n_pages * cache_mla.shape[2]
    db = x_sample.shape[0]
    pos_p = jnp.arange(x_prompt.shape[1], dtype=jnp.int32)
    pos_s = past_len + jnp.arange(x_sample.shape[1], dtype=jnp.int32)
    xp, xs = x_prompt, x_sample
    mla_p, mla_s, diff_p, diff_s, nsa_p, nsa_s, win_p, win_s = [], [], [], [], [], [], [], []
    for i in range(DEPTH):
        lam_init = 0.8 - 0.6 * math.exp(-0.3 * i)
        lw = (w_in[i], mla_g_q[i], mla_w_uq[i], mla_g_kv[i], mla_w_uk[i], mla_w_uv[i],
              diff_lambda[i], diff_g_sub[i], nsa_w_cmp[i], nsa_pos_cmp[i], w_branch[i], w_out[i])
        past = (cache_mla[i][page_table].reshape(db, past_len, MLA_ROW),
                cache_diff[i][page_table].reshape(db, past_len, 2, DIFF_D_V),
                cache_nsa[i][page_table].reshape(db, past_len, 4, NSA_D),
                state_nsa_win[i], past_len)
        outs = []
        for x, c, pos, pst in ((xp, c_prompt, pos_p, None), (xs, c_sample, pos_s, past)):
            sh_a, sc_a, ga_a, sh_f, sc_f, ga_f = adaln(c, w_mod[i], b_mod[i])
            o, r_mla, r_diff, r_nsa, r_win = token_mixers(
                modulate(rms_norm(x, g_attn[i]), sh_a, sc_a), pos, pst, lam_init, *lw)
            x = x + ga_a * o
            h = modulate(rms_norm(x, g_ffn[i]), sh_f, sc_f)
            j = i // 2
            if i % 2 == 0:
                f = swiglu(h, w_gate_dense[j], w_up_dense[j], w_down_dense[j])
            else:
                f = moe_ffn(h, w_router[j], b_router[j], w_gate_exp[j], w_up_exp[j], w_down_exp[j])
            x = x + ga_f * f
            outs.append((x, r_mla, r_diff, r_nsa, r_win))
        xp = outs[0][0]
        xs = outs[1][0]
        mla_p.append(outs[0][1]); diff_p.append(outs[0][2]); nsa_p.append(outs[0][3]); win_p.append(outs[0][4])
        mla_s.append(outs[1][1]); diff_s.append(outs[1][2]); nsa_s.append(outs[1][3]); win_s.append(outs[1][4])
    y_prompt = rms_norm(xp, g_final)
    y_sample = rms_norm(xs, g_final)
    return (y_prompt, y_sample,
            jnp.stack(mla_p), jnp.stack(mla_s),
            jnp.stack(diff_p), jnp.stack(diff_s),
            jnp.stack(nsa_p), jnp.stack(nsa_s),
            jnp.stack(win_p), jnp.stack(win_s))
```

```python
import functools
import math

import jax
import jax.numpy as jnp
from jax import lax
from jax.experimental import pallas as pl
from jax.experimental.pallas import tpu as pltpu

F32 = jnp.float32
BF16 = jnp.bfloat16

MLA_HEADS, MLA_D_NOPE, MLA_D_ROPE, MLA_D_V, MLA_D_C, MLA_D_Q = 8, 64, 32, 64, 128, 384
ROPE_BASE = 10000.0
DIFF_HEADS, DIFF_D = 4, 64
DIFF_D_V = 2 * DIFF_D
NSA_HEADS, NSA_D, NSA_BLOCK, NSA_TOP_N, NSA_WINDOW = 8, 64, 64, 16, 512
N_BRANCH, BRANCH_W = 3, 512
N_EXPERTS, TOP_K = 8, 2
EPS = 1e-6
NEG = -1e30

LANES = 128
V7X_VMEM_LIMIT = 56 * 1024 * 1024

C_DQ, C_NQ, C_CQ, C_CKV, C_DKV, C_NSA, C_WIN, C_TAIL, N_SMALL = 0, 512, 1024, 1408, 1536, 1792, 2048, 2176, 2304
TAIL_KR, TAIL_KROT, TAIL_NG = 0, 32, 64


def _cparams(*sem):
    return pltpu.CompilerParams(dimension_semantics=sem, vmem_limit_bytes=V7X_VMEM_LIMIT)


def _tile(n, pref):
    if n <= pref:
        return n
    t = pref - pref % 8
    while t > 8 and n % t:
        t -= 8
    assert n % t == 0, (n, pref)
    return t


def _sigmoid(x):
    return 1.0 / (1.0 + jnp.exp(-x))


def _dot(a, b):
    return jnp.dot(a, b, preferred_element_type=F32)


def _dot_nt(a, b):
    return lax.dot_general(a, b, (((1,), (1,)), ((), ())), preferred_element_type=F32)


def _rms(x, g):
    return x * lax.rsqrt(jnp.mean(x * x, axis=-1, keepdims=True) + EPS) * g


def _softmax_parts(s, mask, guard_empty=False):
    s = jnp.where(mask, s, NEG)
    m = jnp.max(s, axis=-1, keepdims=True)
    p = jnp.exp(s - m)
    if guard_empty:
        p = jnp.where(mask, p, 0.0)
    l = jnp.sum(p, axis=-1, keepdims=True)
    if guard_empty:
        l = jnp.maximum(l, 1e-30)
    return p, 1.0 / l


def _adaln_kernel(c_ref, w_ref, b_ref, o_ref):
    c = c_ref[...]
    a = (c * _sigmoid(c)).astype(BF16)
    o_ref[...] = _dot(a, w_ref[...].astype(BF16)) + b_ref[...]


def _adaln(c_all, w_mod, b_mod):
    depth, d, n = w_mod.shape
    rows = c_all.shape[0]
    tn = _tile(n, 1024)
    return pl.pallas_call(
        _adaln_kernel,
        grid=(depth, n // tn),
        in_specs=[pl.BlockSpec((rows, d), lambda l, j: (0, 0)),
                  pl.BlockSpec((None, d, tn), lambda l, j: (l, 0, j)),
                  pl.BlockSpec((None, 1, tn), lambda l, j: (l, 0, j))],
        out_specs=pl.BlockSpec((None, rows, tn), lambda l, j: (l, 0, j)),
        out_shape=jax.ShapeDtypeStruct((depth, rows, n), F32),
        compiler_params=_cparams("arbitrary", "arbitrary"),
        name="adaln",
    )(c_all, w_mod, b_mod.reshape(depth, 1, n))


class _Stream:
    def __init__(self, m, tm, mod3, per, d):
        self.m, self.tm, self.mod3, self.per, self.d = m, tm, mod3, per, d

    def mod_spec(self, chunk, row_axis=0):
        r, d, per = self.mod3.shape[1], self.d, self.per
        if row_axis == 0:
            return pl.BlockSpec((None, r, d), lambda i, *_: (i // per, 0, chunk))
        return pl.BlockSpec((None, r, d), lambda j, i: (i // per, 0, chunk))


def _norm_mod_kernel(x_ref, g_ref, sc_ref, sh_ref, o_ref):
    y = _rms(x_ref[...], g_ref[...])
    o_ref[...] = (y * (1.0 + sc_ref[...]) + sh_ref[...]).astype(o_ref.dtype)


def _norm_mod(x, g, st, c_scale, c_shift):
    m, d, tm = st.m, st.d, st.tm
    return pl.pallas_call(
        _norm_mod_kernel,
        grid=(m // tm,),
        in_specs=[pl.BlockSpec((tm, d), lambda i: (i, 0)),
                  pl.BlockSpec((1, d), lambda i: (0, 0)),
                  st.mod_spec(c_scale), st.mod_spec(c_shift)],
        out_specs=pl.BlockSpec((tm, d), lambda i: (i, 0)),
        out_shape=jax.ShapeDtypeStruct((m, d), BF16),
        compiler_params=_cparams("arbitrary"),
        name="norm_mod",
    )(x, g.reshape(1, d), st.mod3, st.mod3)


def _norm_mod_route_kernel(x_ref, g_ref, sc_ref, sh_ref, wr_ref, br_ref, o_ref, gate_ref):
    y = _rms(x_ref[...], g_ref[...])
    h = y * (1.0 + sc_ref[...]) + sh_ref[...]
    o_ref[...] = h.astype(o_ref.dtype)
    logits = jnp.dot(h, wr_ref[...], preferred_element_type=F32,
                     precision=lax.Precision.HIGHEST) + br_ref[...]
    ne = logits.shape[-1]
    ids = lax.broadcasted_iota(jnp.int32, logits.shape, 1)
    m1 = jnp.max(logits, axis=-1, keepdims=True)
    i1 = jnp.min(jnp.where(logits == m1, ids, ne), axis=-1, keepdims=True)
    rest = jnp.where(ids == i1, -jnp.inf, logits)
    m2 = jnp.max(rest, axis=-1, keepdims=True)
    i2 = jnp.min(jnp.where(rest == m2, ids, ne), axis=-1, keepdims=True)
    e = jnp.exp(m2 - m1)
    w1 = 1.0 / (1.0 + e)
    w2 = e * w1
    gates = jnp.where(ids == i1, w1, jnp.where(ids == i2, w2, 0.0))
    for k in range(ne):
        gate_ref[k] = jnp.broadcast_to(gates[:, k:k + 1], gate_ref.shape[1:])


def _norm_mod_route(x, g, st, c_scale, c_shift, w_router, b_router):
    m, d, tm = st.m, st.d, st.tm
    ne = w_router.shape[1]
    return pl.pallas_call(
        _norm_mod_route_kernel,
        grid=(m // tm,),
        in_specs=[pl.BlockSpec((tm, d), lambda i: (i, 0)),
                  pl.BlockSpec((1, d), lambda i: (0, 0)),
                  st.mod_spec(c_scale), st.mod_spec(c_shift),
                  pl.BlockSpec((d, ne), lambda i: (0, 0)),
                  pl.BlockSpec((1, ne), lambda i: (0, 0))],
        out_specs=[pl.BlockSpec((tm, d), lambda i: (i, 0)),
                   pl.BlockSpec((ne, tm, LANES), lambda i: (0, i, 0))],
        out_shape=[jax.ShapeDtypeStruct((m, d), BF16),
                   jax.ShapeDtypeStruct((ne, m, LANES), F32)],
        compiler_params=_cparams("arbitrary"),
        name="norm_mod_route",
    )(x, g.reshape(1, d), st.mod3, st.mod3, w_router, b_router.reshape(1, ne))


def _final_norm_kernel(x_ref, g_ref, o_ref):
    o_ref[...] = _rms(x_ref[...], g_ref[...])


def _final_norm(x, g, tm):
    m, d = x.shape
    return pl.pallas_call(
        _final_norm_kernel,
        grid=(m // tm,),
        in_specs=[pl.BlockSpec((tm, d), lambda i: (i, 0)), pl.BlockSpec((1, d), lambda i: (0, 0))],
        out_specs=pl.BlockSpec((tm, d), lambda i: (i, 0)),
        out_shape=jax.ShapeDtypeStruct((m, d), F32),
        compiler_params=_cparams("arbitrary"),
        name="final_norm",
    )(x, g.reshape(1, d))


def _mm_plain_kernel(x_ref, w_ref, o_ref):
    o_ref[...] = _dot(x_ref[...], w_ref[...]).astype(o_ref.dtype)


def _mm_dual_kernel(x_ref, w_ref, o32_ref, o16_ref):
    acc = _dot(x_ref[...], w_ref[...])
    o32_ref[...] = acc
    o16_ref[...] = acc.astype(BF16)


def _mm_sigmoid_kernel(x_ref, w_ref, o_ref):
    o_ref[...] = _sigmoid(_dot(x_ref[...], w_ref[...])).astype(o_ref.dtype)


def _mm_residual_kernel(x_ref, w_ref, res_ref, gate_ref, o_ref):
    o_ref[...] = res_ref[...] + gate_ref[...] * _dot(x_ref[...], w_ref[...])


def _matmul(x, w, tm, tn, mode="plain", out_dtype=BF16, res=None, st=None, c_gate=None):
    m, k = x.shape
    n = w.shape[1]
    tn = _tile(n, tn)
    in_specs = [pl.BlockSpec((tm, k), lambda j, i: (i, 0)),
                pl.BlockSpec((k, tn), lambda j, i: (0, j))]
    o_spec = pl.BlockSpec((tm, tn), lambda j, i: (i, j))
    args = [x, w]
    if mode == "plain":
        body, out_specs, out_shape = _mm_plain_kernel, o_spec, jax.ShapeDtypeStruct((m, n), out_dtype)
    elif mode == "dual":
        body, out_specs = _mm_dual_kernel, [o_spec, o_spec]
        out_shape = [jax.ShapeDtypeStruct((m, n), F32), jax.ShapeDtypeStruct((m, n), BF16)]
    elif mode == "sigmoid":
        body, out_specs, out_shape = _mm_sigmoid_kernel, o_spec, jax.ShapeDtypeStruct((m, n), out_dtype)
    else:
        assert tn == n == st.d
        body, out_specs, out_shape = _mm_residual_kernel, o_spec, jax.ShapeDtypeStruct((m, n), F32)
        in_specs += [pl.BlockSpec((tm, tn), lambda j, i: (i, j)), st.mod_spec(c_gate, row_axis=1)]
        args += [res, st.mod3]
    return pl.pallas_call(
        body, grid=(n // tn, m // tm), in_specs=in_specs, out_specs=out_specs, out_shape=out_shape,
        compiler_params=_cparams("arbitrary", "arbitrary"), name="mm_" + mode,
    )(*args)


def _mla_prep_kernel(cqkv_ref, tail_ref, gq_ref, gkv_ref, wuq_ref, cos_ref, sin_ref,
                     qn_ref, qr_ref, rows_ref, rows16_ref):
    cqkv = cqkv_ref[...]
    cq, ckv = cqkv[:, :MLA_D_Q], cqkv[:, MLA_D_Q:]
    cos, sin = cos_ref[...], sin_ref[...]
    q = _dot(_rms(cq, gq_ref[...]).astype(BF16), wuq_ref[...])
    n_nope = MLA_HEADS * MLA_D_NOPE
    n_rope = MLA_HEADS * MLA_D_ROPE
    qn_ref[...] = q[:, :n_nope].astype(BF16)
    qr_ref[...] = (q[:, n_nope:n_nope + n_rope] * cos + q[:, n_nope + n_rope:] * sin).astype(BF16)
    tail = tail_ref[...]
    kr = (tail[:, TAIL_KR:TAIL_KR + MLA_D_ROPE] * cos[:, :MLA_D_ROPE]
          + tail[:, TAIL_KROT:TAIL_KROT + MLA_D_ROPE] * sin[:, :MLA_D_ROPE])
    ckvn = _rms(ckv, gkv_ref[...])
    rows_ref[:, :MLA_D_C] = ckvn
    rows_ref[:, MLA_D_C:] = kr
    rows16_ref[:, :MLA_D_C] = ckvn.astype(BF16)
    rows16_ref[:, MLA_D_C:] = kr.astype(BF16)


def _mla_prep(proj32, g_q, g_kv, w_uq_r, cos_t, sin_t, tm):
    m = proj32.shape[0]
    n_pos = cos_t.shape[0] // tm
    n_nope, n_rope = MLA_HEADS * MLA_D_NOPE, MLA_HEADS * MLA_D_ROPE
    row = MLA_D_C + MLA_D_ROPE
    return pl.pallas_call(
        _mla_prep_kernel,
        grid=(m // tm,),
        in_specs=[pl.BlockSpec((tm, 512), lambda i: (i, C_CQ // 512)),
                  pl.BlockSpec((tm, LANES), lambda i: (i, C_TAIL // LANES)),
                  pl.BlockSpec((1, MLA_D_Q), lambda i: (0, 0)),
                  pl.BlockSpec((1, MLA_D_C), lambda i: (0, 0)),
                  pl.BlockSpec(w_uq_r.shape, lambda i: (0, 0)),
                  pl.BlockSpec((tm, n_rope), lambda i: (i % n_pos, 0)),
                  pl.BlockSpec((tm, n_rope), lambda i: (i % n_pos, 0))],
        out_specs=[pl.BlockSpec((tm, n_nope), lambda i: (i, 0)),
                   pl.BlockSpec((tm, n_rope), lambda i: (i, 0)),
                   pl.BlockSpec((tm, row), lambda i: (i, 0)),
                   pl.BlockSpec((tm, row), lambda i: (i, 0))],
        out_shape=[jax.ShapeDtypeStruct((m, n_nope), BF16),
                   jax.ShapeDtypeStruct((m, n_rope), BF16),
                   jax.ShapeDtypeStruct((m, row), F32),
                   jax.ShapeDtypeStruct((m, row), BF16)],
        compiler_params=_cparams("arbitrary"),
        name="mla_prep",
    )(proj32, proj32, g_q.reshape(1, -1), g_kv.reshape(1, -1), w_uq_r, cos_t, sin_t)


def _mla_prompt_kernel(ql_ref, qr_ref, k_ref, o_ref):
    tq = ql_ref.shape[0]
    t = k_ref.shape[0]
    i = pl.program_id(1)
    k = k_ref[...]
    kc, kr = k[:, :MLA_D_C], k[:, MLA_D_C:]
    qpos = i * tq + lax.broadcasted_iota(jnp.int32, (tq, 1), 0)
    kpos = lax.broadcasted_iota(jnp.int32, (1, t), 1)
    mask = kpos <= qpos
    scale = (MLA_D_NOPE + MLA_D_ROPE) ** -0.5
    for h in range(MLA_HEADS):
        ql = ql_ref[:, h * MLA_D_C:(h + 1) * MLA_D_C]
        qr = qr_ref[:, h * MLA_D_ROPE:(h + 1) * MLA_D_ROPE]
        s = (_dot_nt(ql, kc) + _dot_nt(qr, kr)) * scale
        p, inv = _softmax_parts(s, mask)
        o_ref[:, h * MLA_D_C:(h + 1) * MLA_D_C] = (_dot(p.astype(BF16), kc) * inv).astype(BF16)


def _mla_prompt(q_lat, q_rope, rows16, b, t, tq):
    nl, nr, row = MLA_HEADS * MLA_D_C, MLA_HEADS * MLA_D_ROPE, MLA_D_C + MLA_D_ROPE
    out = pl.pallas_call(
        _mla_prompt_kernel,
        grid=(b, t // tq),
        in_specs=[pl.BlockSpec((None, tq, nl), lambda bi, i: (bi, i, 0)),
                  pl.BlockSpec((None, tq, nr), lambda bi, i: (bi, i, 0)),
                  pl.BlockSpec((None, t, row), lambda bi, i: (bi, 0, 0))],
        out_specs=pl.BlockSpec((None, tq, nl), lambda bi, i: (bi, i, 0)),
        out_shape=jax.ShapeDtypeStruct((b, t, nl), BF16),
        compiler_params=_cparams("arbitrary", "arbitrary"),
        name="mla_prompt",
    )(q_lat.reshape(b, t, nl), q_rope.reshape(b, t, nr), rows16.reshape(b, t, row))
    return out.reshape(b * t, nl)


def _diff_lambda(lam_ref, lam_init):
    lam = lam_ref[...]
    a = jnp.sum(lam[0:1] * lam[1:2], axis=-1, keepdims=True)
    b = jnp.sum(lam[2:3] * lam[3:4], axis=-1, keepdims=True)
    return jnp.exp(a) - jnp.exp(b) + lam_init


def _diff_prompt_kernel(q_ref, kv_ref, lam_ref, g_ref, o_ref, *, lam_init):
    tq = q_ref.shape[0]
    t = kv_ref.shape[0]
    i = pl.program_id(1)
    kv = kv_ref[...]
    k12, v = kv[:, :2 * DIFF_D], kv[:, 2 * DIFF_D:]
    lam = _diff_lambda(lam_ref, lam_init)
    qpos = i * tq + lax.broadcasted_iota(jnp.int32, (tq, 1), 0)
    kpos = lax.broadcasted_iota(jnp.int32, (1, t), 1)
    dist = (qpos - kpos).astype(F32)
    mask = dist >= 0
    low = lax.broadcasted_iota(jnp.int32, (tq, 2 * DIFF_D), 1) < DIFF_D
    zero = jnp.zeros((tq, 2 * DIFF_D), BF16)
    scale = DIFF_D ** -0.5
    for h in range(DIFF_HEADS):
        bias = dist * (2.0 ** (-8.0 * (h + 1) / DIFF_HEADS))
        qh = q_ref[:, h * 2 * DIFF_D:(h + 1) * 2 * DIFF_D]
        outs = []
        for half in range(2):
            qm = jnp.where(low if half == 0 else jnp.logical_not(low), qh, zero)
            s = _dot_nt(qm, k12) * scale - bias
            p, inv = _softmax_parts(s, mask)
            outs.append(_dot(p.astype(BF16), v) * inv)
        o = outs[0] - lam * outs[1]
        o = _rms(o, g_ref[...]) * (1.0 - lam_init)
        o_ref[:, h * DIFF_D_V:(h + 1) * DIFF_D_V] = o.astype(BF16)


def _diff_prompt(proj16, lam4, g_sub, lam_init, b, t, tq):
    nq = DIFF_HEADS * 2 * DIFF_D
    p3 = proj16.reshape(b, t, N_SMALL)
    out = pl.pallas_call(
        functools.partial(_diff_prompt_kernel, lam_init=lam_init),
        grid=(b, t // tq),
        in_specs=[pl.BlockSpec((None, tq, nq), lambda bi, i: (bi, i, C_DQ // nq)),
                  pl.BlockSpec((None, t, 256), lambda bi, i: (bi, 0, C_DKV // 256)),
                  pl.BlockSpec((4, DIFF_D), lambda bi, i: (0, 0)),
                  pl.BlockSpec((1, DIFF_D_V), lambda bi, i: (0, 0))],
        out_specs=pl.BlockSpec((None, tq, DIFF_HEADS * DIFF_D_V), lambda bi, i: (bi, i, 0)),
        out_shape=jax.ShapeDtypeStruct((b, t, DIFF_HEADS * DIFF_D_V), BF16),
        compiler_params=_cparams("arbitrary", "arbitrary"),
        name="diff_prompt",
    )(p3, p3, lam4, g_sub.reshape(1, -1))
    return out.reshape(b * t, -1)


def _compress_blocks(load_rows, pos_ref, wr_ref, n_blk):
    acc = jnp.zeros((n_blk, 2 * NSA_D), F32)
    for r in range(NSA_BLOCK):
        x = (load_rows(r) + pos_ref[r:r + 1, :]).astype(BF16)
        acc = acc + _dot(x, wr_ref[r])
    return acc


def _pad_q(q):
    return jnp.concatenate([q, jnp.zeros_like(q)], axis=1)


def _nsa_select(imp, cur, n_sel):
    n_blk = imp.shape[1]
    bid = lax.broadcasted_iota(jnp.int32, imp.shape, 1)
    forced = (bid == 0) | (bid == cur) | (bid == cur - 1)
    imp = jnp.where(forced, NSA_HEADS + 1.0, imp)
    imp = jnp.where(bid > cur, -1.0, imp)
    rank = jnp.zeros(imp.shape, F32)
    for j in range(n_blk):
        vj = imp[:, j:j + 1]
        beats = (vj > imp) | ((vj == imp) & (bid > j))
        rank = rank + jnp.where(beats, 1.0, 0.0)
    return rank < n_sel


def _nsa_prompt_kernel(q_ref, rows32_ref, rows16_ref, win_ref, tail_ref, pos_ref, wr_ref, exp_ref,
                       o_ref, kvc_ref):
    tq = q_ref.shape[0]
    t = rows16_ref.shape[0]
    n_blk = kvc_ref.shape[0]
    i = pl.program_id(1)

    @pl.when(i == 0)
    def _():
        kvc_ref[...] = _compress_blocks(
            lambda r: rows32_ref[pl.ds(r, n_blk, stride=NSA_BLOCK), :], pos_ref, wr_ref, n_blk)

    kvc = kvc_ref[...].astype(BF16)
    ksv = rows16_ref[:, 2 * NSA_D:]
    start = pl.multiple_of(i * tq, tq)
    wkv = win_ref[pl.ds(start, NSA_WINDOW + tq), :]
    qpos = i * tq + lax.broadcasted_iota(jnp.int32, (tq, 1), 0)
    scale = NSA_D ** -0.5
    gate = _sigmoid(tail_ref[:, TAIL_NG:TAIL_NG + 3 * NSA_HEADS])

    blk_end = (lax.broadcasted_iota(jnp.int32, (1, n_blk), 1) + 1) * NSA_BLOCK - 1
    dist_c = (qpos - blk_end).astype(F32)
    mask_c = dist_c >= 0
    qs, o_cs = [], []
    imp = jnp.zeros((tq, n_blk), F32)
    for h in range(NSA_HEADS):
        qh = _pad_q(q_ref[:, h * NSA_D:(h + 1) * NSA_D])
        qs.append(qh)
        s = _dot_nt(qh, kvc) * scale - dist_c * (2.0 ** (-8.0 * (h + 1) / NSA_HEADS))
        p, inv = _softmax_parts(s, mask_c, guard_empty=True)
        p = p * inv
        imp = imp + p
        o_cs.append(_dot(p.astype(BF16), kvc)[:, NSA_D:])
    sel = _nsa_select(imp, qpos // NSA_BLOCK, min(NSA_TOP_N, n_blk))
    sel_keys = _dot(jnp.where(sel, 1.0, 0.0).astype(BF16), exp_ref[...]) > 0.5

    kpos = lax.broadcasted_iota(jnp.int32, (1, t), 1)
    dist_s = (qpos - kpos).astype(F32)
    mask_s = sel_keys & (dist_s >= 0)
    wpos = start - NSA_WINDOW + lax.broadcasted_iota(jnp.int32, (1, NSA_WINDOW + tq), 1)
    dw = qpos - wpos
    dist_w = dw.astype(F32)
    mask_w = (dw >= 0) & (dw < NSA_WINDOW) & (wpos >= 0)
    for h in range(NSA_HEADS):
        slope = 2.0 ** (-8.0 * (h + 1) / NSA_HEADS)
        p, inv = _softmax_parts(_dot_nt(qs[h], ksv) * scale - dist_s * slope, mask_s)
        o_s = (_dot(p.astype(BF16), ksv) * inv)[:, NSA_D:]
        p, inv = _softmax_parts(_dot_nt(qs[h], wkv) * scale - dist_w * slope, mask_w)
        o_w = (_dot(p.astype(BF16), wkv) * inv)[:, NSA_D:]
        o = (gate[:, 3 * h:3 * h + 1] * o_cs[h] + gate[:, 3 * h + 1:3 * h + 2] * o_s
             + gate[:, 3 * h + 2:3 * h + 3] * o_w)
        o_ref[:, h * NSA_D:(h + 1) * NSA_D] = o.astype(BF16)


def _nsa_prompt(proj32, proj16, win_pad16, pos_r, w_r, expand, b, t, tq):
    nq = NSA_HEADS * NSA_D
    n_blk = t // NSA_BLOCK
    p32 = proj32.reshape(b, t, N_SMALL)
    p16 = proj16.reshape(b, t, N_SMALL)
    out = pl.pallas_call(
        _nsa_prompt_kernel,
        grid=(b, t // tq),
        in_specs=[pl.BlockSpec((None, tq, nq), lambda bi, i: (bi, i, C_NQ // nq)),
                  pl.BlockSpec((None, t, LANES), lambda bi, i: (bi, 0, C_NSA // LANES)),
                  pl.BlockSpec((None, t, 256), lambda bi, i: (bi, 0, C_NSA // 256)),
                  pl.BlockSpec((None, t + NSA_WINDOW, 2 * NSA_D), lambda bi, i: (bi, 0, 0)),
                  pl.BlockSpec((None, tq, LANES), lambda bi, i: (bi, i, C_TAIL // LANES)),
                  pl.BlockSpec(pos_r.shape, lambda bi, i: (0, 0)),
                  pl.BlockSpec(w_r.shape, lambda bi, i: (0, 0, 0)),
                  pl.BlockSpec(expand.shape, lambda bi, i: (0, 0))],
        out_specs=pl.BlockSpec((None, tq, nq), lambda bi, i: (bi, i, 0)),
        out_shape=jax.ShapeDtypeStruct((b, t, nq), BF16),
        scratch_shapes=[pltpu.VMEM((n_blk, 2 * NSA_D), F32)],
        compiler_params=_cparams("arbitrary", "arbitrary"),
        name="nsa_prompt",
    )(p16, p32, p16, win_pad16, p32, pos_r, w_r, expand)
    return out.reshape(b * t, nq)


def _page_copy(cache_ref, pt_ref, buf_ref, sem_ref, layer, seq, slot, page):
    rows = cache_ref.shape[2]
    return pltpu.make_async_copy(cache_ref.at[layer, pt_ref[seq, page]],
                                 buf_ref.at[slot, pl.ds(page * rows, rows)],
                                 sem_ref.at[slot])


def _gather_step(cache_ref, pt_ref, buf_ref, sem_ref, layer, n_pages):
    b = pl.program_id(0)
    nb = pl.num_programs(0)
    slot = b % 2

    def start_all(seq, sl):
        def body(p, c):
            _page_copy(cache_ref, pt_ref, buf_ref, sem_ref, layer, seq, sl, p).start()
            return c
        lax.fori_loop(0, n_pages, body, 0)

    @pl.when(b == 0)
    def _():
        start_all(0, 0)

    @pl.when(b + 1 < nb)
    def _():
        start_all(b + 1, 1 - slot)

    def wait_body(p, c):
        _page_copy(cache_ref, pt_ref, buf_ref, sem_ref, layer, b, slot, p).wait()
        return c
    lax.fori_loop(0, n_pages, wait_body, 0)
    return slot


def _new_key_mask(n_rows, rows_per_tok, n_new_pad, n_new):
    tok = lax.broadcasted_iota(jnp.int32, (n_rows, n_new_pad), 0) // rows_per_tok
    j = lax.broadcasted_iota(jnp.int32, (n_rows, n_new_pad), 1)
    return (j <= tok) & (j < n_new), (tok - j).astype(F32)


def _two_part_softmax(s_past, mask_past, s_new, mask_new):
    if mask_past is not None:
        s_past = jnp.where(mask_past, s_past, NEG)
    s_new = jnp.where(mask_new, s_new, NEG)
    m = jnp.maximum(jnp.max(s_past, axis=-1, keepdims=True), jnp.max(s_new, axis=-1, keepdims=True))
    p_past = jnp.exp(s_past - m)
    p_new = jnp.exp(s_new - m)
    l = jnp.sum(p_past, axis=-1, keepdims=True) + jnp.sum(p_new, axis=-1, keepdims=True)
    return p_past, p_new, 1.0 / l


def _mla_decode_kernel(pt_ref, q_ref, new_ref, cache_ref, o_ref, buf_ref, sem_ref, *, layer, n_pages, n_new):
    slot = _gather_step(cache_ref, pt_ref, buf_ref, sem_ref, layer, n_pages)
    k = buf_ref[slot].astype(BF16)
    kn = new_ref[...]
    q = q_ref[...]
    scale = (MLA_D_NOPE + MLA_D_ROPE) ** -0.5
    mask_new, _ = _new_key_mask(q.shape[0], MLA_HEADS, kn.shape[0], n_new)
    p, pn, inv = _two_part_softmax(_dot_nt(q, k) * scale, None, _dot_nt(q, kn) * scale, mask_new)
    o = (_dot(p.astype(BF16), k[:, :MLA_D_C]) + _dot(pn.astype(BF16), kn[:, :MLA_D_C])) * inv
    o_ref[...] = o.astype(BF16)


def _decode_call(body, name, page_table, cache, layer, small_inputs, out_cols, extra_inputs=(), out_rows=None):
    db, n_pages = page_table.shape
    page_shape = cache.shape[2:]
    rows = out_rows or small_inputs[0].shape[1]
    in_specs = [pl.BlockSpec((None,) + a.shape[1:], lambda b, pt, nd=a.ndim: (b,) + (0,) * (nd - 1))
                for a in small_inputs]
    in_specs.append(pl.BlockSpec(memory_space=pl.ANY))
    in_specs += [pl.BlockSpec(a.shape, lambda b, pt, nd=a.ndim: (0,) * nd) for a in extra_inputs]
    grid_spec = pltpu.PrefetchScalarGridSpec(
        num_scalar_prefetch=1,
        grid=(db,),
        in_specs=in_specs,
        out_specs=pl.BlockSpec((None, rows, out_cols), lambda b, pt: (b, 0, 0)),
        scratch_shapes=[pltpu.VMEM((2, n_pages * page_shape[0]) + page_shape[1:], F32),
                        pltpu.SemaphoreType.DMA((2,))],
    )
    return pl.pallas_call(
        functools.partial(body, layer=layer, n_pages=n_pages),
        grid_spec=grid_spec,
        out_shape=jax.ShapeDtypeStruct((db, rows, out_cols), BF16),
        compiler_params=_cparams("arbitrary"),
        name=name,
    )(page_table, *small_inputs, cache, *extra_inputs)


def _diff_decode_kernel(pt_ref, q_ref, new_ref, cache_ref, lam_ref, g_ref, o_ref, buf_ref, sem_ref,
                        *, layer, n_pages, n_new, lam_init, past_len):
    slot = _gather_step(cache_ref, pt_ref, buf_ref, sem_ref, layer, n_pages)
    k12 = buf_ref[slot, :, 0, :].astype(BF16)
    v = buf_ref[slot, :, 1, :].astype(BF16)
    new = new_ref[...]
    kn, vn = new[:, :2 * DIFF_D], new[:, 2 * DIFF_D:]
    q = q_ref[...]
    n_rows = q.shape[0]
    n_half = n_rows // 2
    lk = k12.shape[0]
    r = lax.broadcasted_iota(jnp.int32, (n_rows, 1), 0) % n_half
    tok, head = r // DIFF_HEADS, r % DIFF_HEADS
    slope = jnp.exp2(-8.0 * (head + 1).astype(F32) / DIFF_HEADS)
    dist = (past_len + tok - lax.broadcasted_iota(jnp.int32, (1, lk), 1)).astype(F32)
    tokn = lax.broadcasted_iota(jnp.int32, (n_rows, new.shape[0]), 0) % n_half // DIFF_HEADS
    jn = lax.broadcasted_iota(jnp.int32, (n_rows, new.shape[0]), 1)
    mask_new = (jn <= tokn) & (jn < n_new)
    dist_new = (tokn - jn).astype(F32)
    scale = DIFF_D ** -0.5
    p, pn, inv = _two_part_softmax(_dot_nt(q, k12) * scale - slope * dist, None,
                                   _dot_nt(q, kn) * scale - slope * dist_new, mask_new)
    o = (_dot(p.astype(BF16), v) + _dot(pn.astype(BF16), vn)) * inv
    lam = _diff_lambda(lam_ref, lam_init)
    o = o[:n_half] - lam * o[n_half:]
    o_ref[...] = (_rms(o, g_ref[...]) * (1.0 - lam_init)).astype(BF16)


def _nsa_decode_kernel(pt_ref, q_ref, new_ref, gate_ref, win_ref, cache_ref, pos_ref, wr_ref, exp_ref,
                       o_ref, buf_ref, sem_ref, *, layer, n_pages, n_new, past_len):
    slot = _gather_step(cache_ref, pt_ref, buf_ref, sem_ref, layer, n_pages)
    lk = buf_ref.shape[1]
    n_blk = lk // NSA_BLOCK
    q = _pad_q(q_ref[...])
    n_rows = q.shape[0]
    row = lax.broadcasted_iota(jnp.int32, (n_rows, 1), 0)
    tok, head = row // NSA_HEADS, row % NSA_HEADS
    qpos = past_len + tok
    slope = jnp.exp2(-8.0 * (head + 1).astype(F32) / NSA_HEADS)
    scale = NSA_D ** -0.5
    new = new_ref[...]
    n_new_pad = new.shape[0]
    tokn = lax.broadcasted_iota(jnp.int32, (n_rows, n_new_pad), 0) // NSA_HEADS
    jn = lax.broadcasted_iota(jnp.int32, (n_rows, n_new_pad), 1)
    mask_new = (jn <= tokn) & (jn < n_new)
    dist_new = (tokn - jn).astype(F32)

    def load_rows(r):
        lo = buf_ref[slot, pl.ds(r, n_blk, stride=NSA_BLOCK), 0, :]
        hi = buf_ref[slot, pl.ds(r, n_blk, stride=NSA_BLOCK), 1, :]
        return jnp.concatenate([lo, hi], axis=1)
    kvc = _compress_blocks(load_rows, pos_ref, wr_ref, n_blk).astype(BF16)
    blk_end = (lax.broadcasted_iota(jnp.int32, (1, n_blk), 1) + 1) * NSA_BLOCK - 1
    dist_c = (qpos - blk_end).astype(F32)
    p, inv = _softmax_parts(_dot_nt(q, kvc) * scale - slope * dist_c, dist_c >= 0, guard_empty=True)
    p = p * inv
    o_c = _dot(p.astype(BF16), kvc)[:, NSA_D:]
    n_tok = n_rows // NSA_HEADS
    imp = jnp.sum(p.reshape(n_tok, NSA_HEADS, n_blk), axis=1)

    bid = lax.broadcasted_iota(jnp.int32, (1, n_blk), 1)
    forced = (bid == 0) | (bid == n_blk - 1)
    val = jnp.where(forced, NSA_HEADS + 1.0, imp)
    pad_rows = -n_tok % 8
    val_t = jnp.concatenate([val, jnp.zeros((pad_rows, n_blk), F32)], axis=0).T if pad_rows else val.T
    bi = lax.broadcasted_iota(jnp.int32, (n_blk, n_blk), 0)
    bj = lax.broadcasted_iota(jnp.int32, (n_blk, n_blk), 1)
    n_sel = min(NSA_TOP_N, n_blk + 1)
    sel_rows = []
    for tk in range(n_tok):
        vi = val_t[:, tk:tk + 1]
        vj = val[tk:tk + 1, :]
        beats = (vi > vj) | ((vi == vj) & (bi < bj))
        rank = jnp.sum(jnp.where(beats, 1.0, 0.0), axis=0, keepdims=True)
        rank = rank + jnp.where(vj < NSA_HEADS + 1.0, 1.0, 0.0)
        sel_t = jnp.where(rank < n_sel, 1.0, 0.0)
        sel_rows.append(jnp.broadcast_to(sel_t, (NSA_HEADS, n_blk)))
    sel = jnp.concatenate(sel_rows, axis=0).astype(BF16)
    sel_keys = _dot(sel, exp_ref[...]) > 0.5

    q64 = q_ref[...]
    k_sel = buf_ref[slot, :, 2, :].astype(BF16)
    v_sel = buf_ref[slot, :, 3, :].astype(BF16)
    k_sel_n, v_sel_n = new[:, 2 * NSA_D:3 * NSA_D], new[:, 3 * NSA_D:4 * NSA_D]
    dist_s = (qpos - lax.broadcasted_iota(jnp.int32, (1, lk), 1)).astype(F32)
    p, pn, inv = _two_part_softmax(_dot_nt(q64, k_sel) * scale - slope * dist_s, sel_keys,
                                   _dot_nt(q64, k_sel_n) * scale - slope * dist_new, mask_new)
    o_s = (_dot(p.astype(BF16), v_sel) + _dot(pn.astype(BF16), v_sel_n)) * inv

    wkv = win_ref[...]
    lb = wkv.shape[0]
    wkv_n = new[:, 4 * NSA_D:]
    wpos = past_len - lb + lax.broadcasted_iota(jnp.int32, (1, lb), 1)
    dw = qpos - wpos
    mask_w = (dw >= 0) & (dw < NSA_WINDOW)
    p, pn, inv = _two_part_softmax(_dot_nt(q, wkv) * scale - slope * dw.astype(F32), mask_w,
                                   _dot_nt(q, wkv_n) * scale - slope * dist_new, mask_new)
    o_w = ((_dot(p.astype(BF16), wkv) + _dot(pn.astype(BF16), wkv_n)) * inv)[:, NSA_D:]

    gate = _sigmoid(gate_ref[...])
    o = gate[:, 0:1] * o_c + gate[:, 1:2] * o_s + gate[:, 2:3] * o_w
    o_ref[...] = o.astype(BF16)


def _branch_kernel(o0_ref, o1_ref, o2_ref, g_ref, w_ref, out_ref):
    d = out_ref.shape[1]
    acc = None
    for n, o_ref in enumerate((o0_ref, o1_ref, o2_ref)):
        term = g_ref[:, n * d:(n + 1) * d].astype(F32) * _dot(o_ref[...], w_ref[n])
        acc = term if acc is None else acc + term
    out_ref[...] = acc.astype(BF16)


def _branch_merge(o_mla, o_diff, o_nsa, gates16, w_branch16, tm):
    m = o_mla.shape[0]
    d = w_branch16.shape[2]
    o_spec = pl.BlockSpec((tm, BRANCH_W), lambda i: (i, 0))
    return pl.pallas_call(
        _branch_kernel,
        grid=(m // tm,),
        in_specs=[o_spec, o_spec, o_spec,
                  pl.BlockSpec((tm, N_BRANCH * d), lambda i: (i, 0)),
                  pl.BlockSpec(w_branch16.shape, lambda i: (0, 0, 0))],
        out_specs=pl.BlockSpec((tm, d), lambda i: (i, 0)),
        out_shape=jax.ShapeDtypeStruct((m, d), BF16),
        compiler_params=_cparams("arbitrary"),
        name="branch_merge",
    )(o_mla, o_diff, o_nsa, gates16, w_branch16)


def _ffn_kernel(*refs, moe):
    if moe:
        h_ref, wg_ref, wu_ref, wd_ref, x_ref, gate_ref, eg_ref, o_ref, acc_ref = refs
    else:
        h_ref, wg_ref, wu_ref, wd_ref, x_ref, gate_ref, o_ref, acc_ref = refs
    j = pl.program_id(1)

    @pl.when(j == 0)
    def _():
        acc_ref[...] = jnp.zeros_like(acc_ref)

    h = h_ref[...]
    g = _dot(h, wg_ref[...])
    a = g * _sigmoid(g) * _dot(h, wu_ref[...])
    if moe:
        eg = eg_ref[...]
        a = a * jnp.concatenate([eg] * (a.shape[1] // LANES), axis=1)
    acc_ref[...] += _dot(a.astype(BF16), wd_ref[...])

    @pl.when(j == pl.num_programs(1) - 1)
    def _():
        o_ref[...] = x_ref[...] + gate_ref[...] * acc_ref[...]


def _ffn(h, wg, wu, wd, x, st, c_gate, tf, expert_gates=None):
    m, d, tm = st.m, st.d, st.tm
    ne, _, f = wg.shape
    per_e = f // tf
    moe = expert_gates is not None
    in_specs = [pl.BlockSpec((tm, d), lambda i, j: (i, 0)),
                pl.BlockSpec((None, d, tf), lambda i, j: (j // per_e, 0, j % per_e)),
                pl.BlockSpec((None, d, tf), lambda i, j: (j // per_e, 0, j % per_e)),
                pl.BlockSpec((None, tf, d), lambda i, j: (j // per_e, j % per_e, 0)),
                pl.BlockSpec((tm, d), lambda i, j: (i, 0)),
                st.mod_spec(c_gate)]
    args = [h, wg, wu, wd, x, st.mod3]
    if moe:
        in_specs.append(pl.BlockSpec((None, tm, LANES), lambda i, j: (j // per_e, i, 0)))
        args.append(expert_gates)
    return pl.pallas_call(
        functools.partial(_ffn_kernel, moe=moe),
        grid=(m // tm, ne * per_e),
        in_specs=in_specs,
        out_specs=pl.BlockSpec((tm, d), lambda i, j: (i, 0)),
        out_shape=jax.ShapeDtypeStruct((m, d), F32),
        scratch_shapes=[pltpu.VMEM((tm, d), F32)],
        compiler_params=_cparams("arbitrary", "arbitrary"),
        name="ffn_moe" if moe else "ffn_dense",
    )(*args)


def _rot_cols(w):
    half = w.shape[-1] // 2
    return jnp.concatenate([-w[..., half:], w[..., :half]], axis=-1)


def _prep_w_in(w):
    d = w.shape[0]
    sizes = (MLA_D_Q, MLA_D_C, MLA_D_ROPE, DIFF_HEADS * 2 * DIFF_D, 2 * DIFF_D, DIFF_D_V,
             NSA_HEADS * NSA_D, 6 * NSA_D, 3 * NSA_HEADS, N_BRANCH * d)
    parts, start = [], 0
    for s in sizes:
        parts.append(w[:, start:start + s])
        start += s
    cq, ckv, kr, dq, dk, dv, nq, nkv, ng, mg = parts
    pad = jnp.zeros((d, N_SMALL - C_TAIL - 2 * MLA_D_ROPE - 3 * NSA_HEADS), w.dtype)
    small = jnp.concatenate([dq, nq, cq, ckv, dk, dv, nkv, kr, _rot_cols(kr), ng, pad], axis=1)
    return small.astype(BF16), mg.astype(BF16)


def _prep_w_uq(w_uq):
    w = w_uq.reshape(MLA_D_Q, MLA_HEADS, MLA_D_NOPE + MLA_D_ROPE)
    nope = w[:, :, :MLA_D_NOPE].reshape(MLA_D_Q, -1)
    rope = w[:, :, MLA_D_NOPE:]
    return jnp.concatenate([nope, rope.reshape(MLA_D_Q, -1), _rot_cols(rope).reshape(MLA_D_Q, -1)],
                           axis=1).astype(BF16)


def _block_diag(blocks):
    h, r, c = blocks.shape
    eye = jnp.eye(h, dtype=blocks.dtype)
    return (eye[:, None, :, None] * blocks[:, :, None, :]).reshape(h * r, h * c)


def _rope_tables(pos, reps):
    half = MLA_D_ROPE // 2
    freqs = ROPE_BASE ** (-jnp.arange(half, dtype=F32) / half)
    ang = pos.astype(F32)[:, None] * freqs
    cos = jnp.concatenate([jnp.cos(ang), jnp.cos(ang)], axis=1)
    sin = jnp.concatenate([jnp.sin(ang), jnp.sin(ang)], axis=1)
    return jnp.tile(cos, (1, reps)), jnp.tile(sin, (1, reps))


def _prep_cmp(w_cmp, pos_cmp):
    wk = w_cmp[0].reshape(NSA_BLOCK, NSA_D, NSA_D)
    wv = w_cmp[1].reshape(NSA_BLOCK, NSA_D, NSA_D)
    z = jnp.zeros_like(wk)
    w_r = jnp.concatenate([jnp.concatenate([wk, z], axis=2), jnp.concatenate([z, wv], axis=2)], axis=1)
    pos_r = jnp.concatenate([pos_cmp[0], pos_cmp[1]], axis=1)
    return w_r.astype(BF16), pos_r


def _expand_matrix(n_blk, n_keys):
    return (jnp.arange(n_keys)[None, :] // NSA_BLOCK == jnp.arange(n_blk)[:, None]).astype(BF16)


def _pad_rows(a, rows):
    return jnp.pad(a, ((0, 0), (0, rows - a.shape[1]), (0, 0)))


def kernel(x_prompt, x_sample, cache_mla, cache_diff, cache_nsa, state_nsa_win, page_table, c_prompt, c_sample, w_mod, b_mod, g_attn, g_ffn, w_in, mla_g_q, mla_w_uq, mla_g_kv, mla_w_uk, mla_w_uv, diff_lambda, diff_g_sub, nsa_w_cmp, nsa_pos_cmp, w_branch, w_out, w_gate_dense, w_up_dense, w_down_dense, w_router, b_router, w_gate_exp, w_up_exp, w_down_exp, g_final):
    b, t, d = x_prompt.shape
    db, dt, _ = x_sample.shape
    depth = w_mod.shape[0]
    n_pages, page = page_table.shape[1], cache_mla.shape[2]
    past_len = n_pages * page
    mp, ms = b * t, db * dt
    tm_p, tm_s = _tile(t, 512), _tile(ms, 256)
    tq = _tile(t, 256)
    new_pad = 8

    rows_c = -(-(b + db) // 8) * 8
    c_all = jnp.pad(jnp.concatenate([c_prompt, c_sample], axis=0), ((0, rows_c - b - db), (0, 0)))
    mod = _adaln(c_all, w_mod, b_mod)

    cos_p, sin_p = _rope_tables(jnp.arange(t), MLA_HEADS)
    cos_s, sin_s = _rope_tables(past_len + jnp.arange(dt), MLA_HEADS)
    cos_s, sin_s = jnp.tile(cos_s, (db, 1)), jnp.tile(sin_s, (db, 1))
    exp_p = _expand_matrix(t // NSA_BLOCK, t)
    exp_s = _expand_matrix(past_len // NSA_BLOCK, past_len)
    tf = 512

    xp, xs = x_prompt.reshape(mp, d), x_sample.reshape(ms, d)
    outs = {k: [] for k in ("mla_p", "mla_s", "diff_p", "diff_s", "nsa_p", "nsa_s", "win_p", "win_s")}
    for i in range(depth):
        lam_init = 0.8 - 0.6 * math.exp(-0.3 * i)
        w_small, w_mg = _prep_w_in(w_in[i])
        w_uq_r = _prep_w_uq(mla_w_uq[i])
        w_uk_bd = _block_diag(jnp.transpose(mla_w_uk[i], (1, 2, 0))).astype(BF16)
        w_uv_bd = _block_diag(jnp.transpose(mla_w_uv[i], (1, 0, 2))).astype(BF16)
        w_r, pos_r = _prep_cmp(nsa_w_cmp[i], nsa_pos_cmp[i])
        w_branch16, w_out16 = w_branch[i].astype(BF16), w_out[i].astype(BF16)
        st_p = _Stream(mp, tm_p, mod[i, :b].reshape(b, 1, 6 * d), t // tm_p, d)
        st_s = _Stream(ms, tm_s, jnp.repeat(mod[i, b:b + db], dt, axis=0).reshape(ms // tm_s, tm_s, 6 * d), 1, d)

        new_x = []
        for x, st, is_prompt in ((xp, st_p, True), (xs, st_s, False)):
            tm = st.tm
            h = _norm_mod(x, g_attn[i], st, 1, 0)
            proj32, proj16 = _matmul(h, w_small, tm, N_SMALL, mode="dual")
            gates16 = _matmul(h, w_mg, tm, 1024, mode="sigmoid")
            q_nope, q_rope, mla_rows, mla_rows16 = _mla_prep(
                proj32, mla_g_q[i], mla_g_kv[i], w_uq_r,
                cos_p if is_prompt else cos_s, sin_p if is_prompt else sin_s, tm)
            q_lat = _matmul(q_nope, w_uk_bd, tm, w_uk_bd.shape[1])
            diff_rows = proj32[:, C_DKV:C_DKV + 2 * DIFF_D_V]
            nsa_rows = proj32[:, C_NSA:C_NSA + 4 * NSA_D]
            win_rows = proj32[:, C_WIN:C_WIN + 2 * NSA_D]
            if is_prompt:
                o_lat = _mla_prompt(q_lat, q_rope, mla_rows16, b, t, tq)
                o_diff = _diff_prompt(proj16, diff_lambda[i], diff_g_sub[i], lam_init, b, t, tq)
                win16 = proj16[:, C_WIN:C_WIN + 2 * NSA_D].reshape(b, t, 2 * NSA_D)
                win_pad16 = jnp.pad(win16, ((0, 0), (NSA_WINDOW, 0), (0, 0)))
                o_nsa = _nsa_prompt(proj32, proj16, win_pad16, pos_r, w_r, exp_p, b, t, tq)
                outs["mla_p"].append(mla_rows.reshape(b, t, -1))
                outs["diff_p"].append(diff_rows.reshape(b, t, 2, DIFF_D_V))
                outs["nsa_p"].append(nsa_rows.reshape(b, t, 4, NSA_D))
                win = win_rows.reshape(b, t, 2, NSA_D)
                outs["win_p"].append(win[:, t - min(NSA_WINDOW, t):])
            else:
                nh = MLA_HEADS
                q_mla = jnp.concatenate([q_lat.reshape(ms, nh, MLA_D_C), q_rope.reshape(ms, nh, MLA_D_ROPE)],
                                        axis=2).reshape(db, dt * nh, MLA_D_C + MLA_D_ROPE)
                new_mla = _pad_rows(mla_rows16.reshape(db, dt, -1), new_pad)
                o_lat = _decode_call(
                    functools.partial(_mla_decode_kernel, n_new=dt), "mla_decode", page_table, cache_mla, i,
                    [q_mla, new_mla], MLA_D_C).reshape(ms, nh * MLA_D_C)
                dq = proj16[:, C_DQ:C_DQ + DIFF_HEADS * 2 * DIFF_D].reshape(db, dt * DIFF_HEADS, 2, DIFF_D)
                zq = jnp.zeros_like(dq[:, :, 0])
                q_diff = jnp.concatenate([jnp.concatenate([dq[:, :, 0], zq], axis=2),
                                          jnp.concatenate([zq, dq[:, :, 1]], axis=2)], axis=1)
                new_diff = _pad_rows(proj16[:, C_DKV:C_DKV + 2 * DIFF_D_V].reshape(db, dt, -1), new_pad)
                o_diff = _decode_call(
                    functools.partial(_diff_decode_kernel, n_new=dt, lam_init=lam_init, past_len=past_len),
                    "diff_decode", page_table, cache_diff, i, [q_diff, new_diff], DIFF_D_V,
                    extra_inputs=[diff_lambda[i], diff_g_sub[i].reshape(1, -1)],
                    out_rows=dt * DIFF_HEADS).reshape(ms, DIFF_HEADS * DIFF_D_V)
                q_nsa = proj16[:, C_NQ:C_NQ + NSA_HEADS * NSA_D].reshape(db, dt * NSA_HEADS, NSA_D)
                new_nsa = _pad_rows(proj16[:, C_NSA:C_NSA + 6 * NSA_D].reshape(db, dt, -1), new_pad)
                ng = proj32[:, C_TAIL + TAIL_NG:C_TAIL + TAIL_NG + 3 * NSA_HEADS].reshape(db, dt * NSA_HEADS, 3)
                lb = state_nsa_win.shape[2]
                win16 = state_nsa_win[i].reshape(db, lb, 2 * NSA_D).astype(BF16)
                o_nsa = _decode_call(
                    functools.partial(_nsa_decode_kernel, n_new=dt, past_len=past_len),
                    "nsa_decode", page_table, cache_nsa, i, [q_nsa, new_nsa, ng, win16], NSA_D,
                    extra_inputs=[pos_r, w_r, exp_s]).reshape(ms, NSA_HEADS * NSA_D)
                outs["mla_s"].append(mla_rows.reshape(db, dt, -1))
                outs["diff_s"].append(diff_rows.reshape(db, dt, 2, DIFF_D_V))
                outs["nsa_s"].append(nsa_rows.reshape(db, dt, 4, NSA_D))
                win_all = jnp.concatenate([state_nsa_win[i], win_rows.reshape(db, dt, 2, NSA_D)], axis=1)
                outs["win_s"].append(win_all[:, win_all.shape[1] - min(NSA_WINDOW, win_all.shape[1]):])
            o_mla = _matmul(o_lat, w_uv_bd, tm, w_uv_bd.shape[1])
            merged = _branch_merge(o_mla, o_diff, o_nsa, gates16, w_branch16, tm)
            x = _matmul(merged, w_out16, tm, d, mode="residual", res=x, st=st, c_gate=2)
            j = i // 2
            if i % 2 == 0:
                hf = _norm_mod(x, g_ffn[i], st, 4, 3)
                x = _ffn(hf, w_gate_dense[j].astype(BF16)[None], w_up_dense[j].astype(BF16)[None],
                         w_down_dense[j].astype(BF16)[None], x, st, 5, tf)
            else:
                hf, eg = _norm_mod_route(x, g_ffn[i], st, 4, 3, w_router[j], b_router[j])
                fe = w_gate_exp.shape[3]
                fe_pad = -(-fe // tf) * tf
                wg = jnp.pad(w_gate_exp[j], ((0, 0), (0, 0), (0, fe_pad - fe))).astype(BF16)
                wu = jnp.pad(w_up_exp[j], ((0, 0), (0, 0), (0, fe_pad - fe))).astype(BF16)
                wd = jnp.pad(w_down_exp[j], ((0, 0), (0, fe_pad - fe), (0, 0))).astype(BF16)
                x = _ffn(hf, wg, wu, wd, x, st, 5, tf, expert_gates=eg)
            new_x.append(x)
        xp, xs = new_x

    y_p = _final_norm(xp, g_final, tm_p).reshape(b, t, d)
    y_s = _final_norm(xs, g_final, tm_s).reshape(db, dt, d)
    return (y_p, y_s,
            jnp.stack(outs["mla_p"]), jnp.stack(outs["mla_s"]),
            jnp.stack(outs["diff_p"]), jnp.stack(outs["diff_s"]),
            jnp.stack(outs["nsa_p"]), jnp.stack(outs["nsa_s"]),
            jnp.stack(outs["win_p"]), jnp.stack(outs["win_s"]))
```

```python
import functools
import math

import jax
import jax.numpy as jnp
from jax import lax
from jax.experimental import pallas as pl
from jax.experimental.pallas import tpu as pltpu

F32 = jnp.float32
BF16 = jnp.bfloat16

MLA_HEADS, MLA_D_NOPE, MLA_D_ROPE, MLA_D_V, MLA_D_C, MLA_D_Q = 8, 64, 32, 64, 128, 384
ROPE_BASE = 10000.0
DIFF_HEADS, DIFF_D = 4, 64
DIFF_D_V = 2 * DIFF_D
NSA_HEADS, NSA_D, NSA_BLOCK, NSA_TOP_N, NSA_WINDOW = 8, 64, 64, 16, 512
N_BRANCH, BRANCH_W = 3, 512
N_EXPERTS, TOP_K = 8, 2
EPS = 1e-6
NEG = -1e30

LANES = 128
V7X_VMEM_LIMIT = 56 * 1024 * 1024

C_DQ, C_NQ, C_CQ, C_CKV, C_DKV, C_NSA, C_WIN, C_TAIL, N_SMALL = 0, 512, 1024, 1408, 1536, 1792, 2048, 2176, 2304
TAIL_KR, TAIL_KROT, TAIL_NG = 0, 32, 64


def _cparams(*sem):
    return pltpu.CompilerParams(dimension_semantics=sem, vmem_limit_bytes=V7X_VMEM_LIMIT)


def _tile(n, pref):
    if n <= pref:
        return n
    t = pref - pref % 8
    while t > 8 and n % t:
        t -= 8
    assert n % t == 0, (n, pref)
    return t


def _sigmoid(x):
    return 1.0 / (1.0 + jnp.exp(-x))


def _dot(a, b):
    return jnp.dot(a, b, preferred_element_type=F32)


def _dot_nt(a, b):
    return lax.dot_general(a, b, (((1,), (1,)), ((), ())), preferred_element_type=F32)


def _rms(x, g):
    return x * lax.rsqrt(jnp.mean(x * x, axis=-1, keepdims=True) + EPS) * g


def _softmax_parts(s, mask, guard_empty=False):
    s = jnp.where(mask, s, NEG)
    m = jnp.max(s, axis=-1, keepdims=True)
    p = jnp.exp(s - m)
    if guard_empty:
        p = jnp.where(mask, p, 0.0)
    l = jnp.sum(p, axis=-1, keepdims=True)
    if guard_empty:
        l = jnp.maximum(l, 1e-30)
    return p, 1.0 / l


def _adaln_kernel(c_ref, w_ref, b_ref, o_ref):
    c = c_ref[...]
    a = (c * _sigmoid(c)).astype(BF16)
    o_ref[...] = _dot(a, w_ref[...].astype(BF16)) + b_ref[...]


def _adaln(c_all, w_mod, b_mod):
    depth, d, n = w_mod.shape
    rows = c_all.shape[0]
    tn = _tile(n, 1024)
    return pl.pallas_call(
        _adaln_kernel,
        grid=(depth, n // tn),
        in_specs=[pl.BlockSpec((rows, d), lambda l, j: (0, 0)),
                  pl.BlockSpec((None, d, tn), lambda l, j: (l, 0, j)),
                  pl.BlockSpec((None, 1, tn), lambda l, j: (l, 0, j))],
        out_specs=pl.BlockSpec((None, rows, tn), lambda l, j: (l, 0, j)),
        out_shape=jax.ShapeDtypeStruct((depth, rows, n), F32),
        compiler_params=_cparams("arbitrary", "arbitrary"),
        name="adaln",
    )(c_all, w_mod, b_mod.reshape(depth, 1, n))


class _Stream:
    def __init__(self, m, tm, mod3, per, d):
        self.m, self.tm, self.mod3, self.per, self.d = m, tm, mod3, per, d

    def mod_spec(self, chunk, row_axis=0):
        r, d, per = self.mod3.shape[1], self.d, self.per
        if row_axis == 0:
            return pl.BlockSpec((None, r, d), lambda i, *_: (i // per, 0, chunk))
        return pl.BlockSpec((None, r, d), lambda j, i: (i // per, 0, chunk))


def _norm_mod_kernel(x_ref, g_ref, sc_ref, sh_ref, o_ref):
    y = _rms(x_ref[...], g_ref[...])
    o_ref[...] = (y * (1.0 + sc_ref[...]) + sh_ref[...]).astype(o_ref.dtype)


def _norm_mod(x, g, st, c_scale, c_shift):
    m, d, tm = st.m, st.d, st.tm
    return pl.pallas_call(
        _norm_mod_kernel,
        grid=(m // tm,),
        in_specs=[pl.BlockSpec((tm, d), lambda i: (i, 0)),
                  pl.BlockSpec((1, d), lambda i: (0, 0)),
                  st.mod_spec(c_scale), st.mod_spec(c_shift)],
        out_specs=pl.BlockSpec((tm, d), lambda i: (i, 0)),
        out_shape=jax.ShapeDtypeStruct((m, d), BF16),
        compiler_params=_cparams("arbitrary"),
        name="norm_mod",
    )(x, g.reshape(1, d), st.mod3, st.mod3)


def _norm_mod_route_kernel(x_ref, g_ref, sc_ref, sh_ref, wr_ref, br_ref, o_ref, gate_ref):
    y = _rms(x_ref[...], g_ref[...])
    h = y * (1.0 + sc_ref[...]) + sh_ref[...]
    o_ref[...] = h.astype(o_ref.dtype)
    logits = jnp.dot(h, wr_ref[...], preferred_element_type=F32,
                     precision=lax.Precision.HIGHEST) + br_ref[...]
    ne = logits.shape[-1]
    ids = lax.broadcasted_iota(jnp.int32, logits.shape, 1)
    m1 = jnp.max(logits, axis=-1, keepdims=True)
    i1 = jnp.min(jnp.where(logits == m1, ids, ne), axis=-1, keepdims=True)
    rest = jnp.where(ids == i1, -jnp.inf, logits)
    m2 = jnp.max(rest, axis=-1, keepdims=True)
    i2 = jnp.min(jnp.where(rest == m2, ids, ne), axis=-1, keepdims=True)
    e = jnp.exp(m2 - m1)
    w1 = 1.0 / (1.0 + e)
    w2 = e * w1
    gates = jnp.where(ids == i1, w1, jnp.where(ids == i2, w2, 0.0))
    for k in range(ne):
        gate_ref[k] = jnp.broadcast_to(gates[:, k:k + 1], gate_ref.shape[1:])


def _norm_mod_route(x, g, st, c_scale, c_shift, w_router, b_router):
    m, d, tm = st.m, st.d, st.tm
    ne = w_router.shape[1]
    return pl.pallas_call(
        _norm_mod_route_kernel,
        grid=(m // tm,),
        in_specs=[pl.BlockSpec((tm, d), lambda i: (i, 0)),
                  pl.BlockSpec((1, d), lambda i: (0, 0)),
                  st.mod_spec(c_scale), st.mod_spec(c_shift),
                  pl.BlockSpec((d, ne), lambda i: (0, 0)),
                  pl.BlockSpec((1, ne), lambda i: (0, 0))],
        out_specs=[pl.BlockSpec((tm, d), lambda i: (i, 0)),
                   pl.BlockSpec((ne, tm, LANES), lambda i: (0, i, 0))],
        out_shape=[jax.ShapeDtypeStruct((m, d), BF16),
                   jax.ShapeDtypeStruct((ne, m, LANES), F32)],
        compiler_params=_cparams("arbitrary"),
        name="norm_mod_route",
    )(x, g.reshape(1, d), st.mod3, st.mod3, w_router, b_router.reshape(1, ne))


def _final_norm_kernel(x_ref, g_ref, o_ref):
    o_ref[...] = _rms(x_ref[...], g_ref[...])


def _final_norm(x, g, tm):
    m, d = x.shape
    return pl.pallas_call(
        _final_norm_kernel,
        grid=(m // tm,),
        in_specs=[pl.BlockSpec((tm, d), lambda i: (i, 0)), pl.BlockSpec((1, d), lambda i: (0, 0))],
        out_specs=pl.BlockSpec((tm, d), lambda i: (i, 0)),
        out_shape=jax.ShapeDtypeStruct((m, d), F32),
        compiler_params=_cparams("arbitrary"),
        name="final_norm",
    )(x, g.reshape(1, d))


def _mm_plain_kernel(x_ref, w_ref, o_ref):
    o_ref[...] = _dot(x_ref[...], w_ref[...]).astype(o_ref.dtype)


def _mm_dual_kernel(x_ref, w_ref, o32_ref, o16_ref):
    acc = _dot(x_ref[...], w_ref[...])
    o32_ref[...] = acc
    o16_ref[...] = acc.astype(BF16)


def _mm_sigmoid_kernel(x_ref, w_ref, o_ref):
    o_ref[...] = _sigmoid(_dot(x_ref[...], w_ref[...])).astype(o_ref.dtype)


def _mm_residual_kernel(x_ref, w_ref, res_ref, gate_ref, o_ref):
    o_ref[...] = res_ref[...] + gate_ref[...] * _dot(x_ref[...], w_ref[...])


def _matmul(x, w, tm, tn, mode="plain", out_dtype=BF16, res=None, st=None, c_gate=None):
    m, k = x.shape
    n = w.shape[1]
    tn = _tile(n, tn)
    in_specs = [pl.BlockSpec((tm, k), lambda j, i: (i, 0)),
                pl.BlockSpec((k, tn), lambda j, i: (0, j))]
    o_spec = pl.BlockSpec((tm, tn), lambda j, i: (i, j))
    args = [x, w]
    if mode == "plain":
        body, out_specs, out_shape = _mm_plain_kernel, o_spec, jax.ShapeDtypeStruct((m, n), out_dtype)
    elif mode == "dual":
        body, out_specs = _mm_dual_kernel, [o_spec, o_spec]
        out_shape = [jax.ShapeDtypeStruct((m, n), F32), jax.ShapeDtypeStruct((m, n), BF16)]
    elif mode == "sigmoid":
        body, out_specs, out_shape = _mm_sigmoid_kernel, o_spec, jax.ShapeDtypeStruct((m, n), out_dtype)
    else:
        assert tn == n == st.d
        body, out_specs, out_shape = _mm_residual_kernel, o_spec, jax.ShapeDtypeStruct((m, n), F32)
        in_specs += [pl.BlockSpec((tm, tn), lambda j, i: (i, j)), st.mod_spec(c_gate, row_axis=1)]
        args += [res, st.mod3]
    return pl.pallas_call(
        body, grid=(n // tn, m // tm), in_specs=in_specs, out_specs=out_specs, out_shape=out_shape,
        compiler_params=_cparams("arbitrary", "arbitrary"), name="mm_" + mode,
    )(*args)


def _mla_prep_kernel(cqkv_ref, tail_ref, gq_ref, gkv_ref, wuq_ref, cos_ref, sin_ref,
                     qn_ref, qr_ref, rows_ref, rows16_ref):
    cqkv = cqkv_ref[...]
    cq, ckv = cqkv[:, :MLA_D_Q], cqkv[:, MLA_D_Q:]
    cos, sin = cos_ref[...], sin_ref[...]
    q = _dot(_rms(cq, gq_ref[...]).astype(BF16), wuq_ref[...])
    n_nope = MLA_HEADS * MLA_D_NOPE
    n_rope = MLA_HEADS * MLA_D_ROPE
    qn_ref[...] = q[:, :n_nope].astype(BF16)
    qr_ref[...] = (q[:, n_nope:n_nope + n_rope] * cos + q[:, n_nope + n_rope:] * sin).astype(BF16)
    tail = tail_ref[...]
    kr = (tail[:, TAIL_KR:TAIL_KR + MLA_D_ROPE] * cos[:, :MLA_D_ROPE]
          + tail[:, TAIL_KROT:TAIL_KROT + MLA_D_ROPE] * sin[:, :MLA_D_ROPE])
    ckvn = _rms(ckv, gkv_ref[...])
    rows_ref[:, :MLA_D_C] = ckvn
    rows_ref[:, MLA_D_C:] = kr
    rows16_ref[:, :MLA_D_C] = ckvn.astype(BF16)
    rows16_ref[:, MLA_D_C:] = kr.astype(BF16)


def _mla_prep(proj32, g_q, g_kv, w_uq_r, cos_t, sin_t, tm):
    m = proj32.shape[0]
    n_pos = cos_t.shape[0] // tm
    n_nope, n_rope = MLA_HEADS * MLA_D_NOPE, MLA_HEADS * MLA_D_ROPE
    row = MLA_D_C + MLA_D_ROPE
    return pl.pallas_call(
        _mla_prep_kernel,
        grid=(m // tm,),
        in_specs=[pl.BlockSpec((tm, 512), lambda i: (i, C_CQ // 512)),
                  pl.BlockSpec((tm, LANES), lambda i: (i, C_TAIL // LANES)),
                  pl.BlockSpec((1, MLA_D_Q), lambda i: (0, 0)),
                  pl.BlockSpec((1, MLA_D_C), lambda i: (0, 0)),
                  pl.BlockSpec(w_uq_r.shape, lambda i: (0, 0)),
                  pl.BlockSpec((tm, n_rope), lambda i: (i % n_pos, 0)),
                  pl.BlockSpec((tm, n_rope), lambda i: (i % n_pos, 0))],
        out_specs=[pl.BlockSpec((tm, n_nope), lambda i: (i, 0)),
                   pl.BlockSpec((tm, n_rope), lambda i: (i, 0)),
                   pl.BlockSpec((tm, row), lambda i: (i, 0)),
                   pl.BlockSpec((tm, row), lambda i: (i, 0))],
        out_shape=[jax.ShapeDtypeStruct((m, n_nope), BF16),
                   jax.ShapeDtypeStruct((m, n_rope), BF16),
                   jax.ShapeDtypeStruct((m, row), F32),
                   jax.ShapeDtypeStruct((m, row), BF16)],
        compiler_params=_cparams("arbitrary"),
        name="mla_prep",
    )(proj32, proj32, g_q.reshape(1, -1), g_kv.reshape(1, -1), w_uq_r, cos_t, sin_t)


CAUSAL_CLASSES = 4


def _for_causal_prefix(i, n_q_blocks, t, body):
    n_cls = CAUSAL_CLASSES if n_q_blocks % CAUSAL_CLASSES == 0 else 1
    cls = i * n_cls // n_q_blocks
    for c in range(n_cls):
        pl.when(cls == c)(functools.partial(body, t * (c + 1) // n_cls))


def _mla_prompt_kernel(ql_ref, qr_ref, k_ref, o_ref):
    tq = ql_ref.shape[0]
    t = k_ref.shape[0]
    i = pl.program_id(1)
    scale = (MLA_D_NOPE + MLA_D_ROPE) ** -0.5

    def body(kv_len):
        kc, kr = k_ref[0:kv_len, :MLA_D_C], k_ref[0:kv_len, MLA_D_C:]
        qpos = i * tq + lax.broadcasted_iota(jnp.int32, (tq, 1), 0)
        kpos = lax.broadcasted_iota(jnp.int32, (1, kv_len), 1)
        mask = kpos <= qpos
        for h in range(MLA_HEADS):
            ql = ql_ref[:, h * MLA_D_C:(h + 1) * MLA_D_C]
            qr = qr_ref[:, h * MLA_D_ROPE:(h + 1) * MLA_D_ROPE]
            s = (_dot_nt(ql, kc) + _dot_nt(qr, kr)) * scale
            p, inv = _softmax_parts(s, mask)
            o_ref[:, h * MLA_D_C:(h + 1) * MLA_D_C] = (_dot(p.astype(BF16), kc) * inv).astype(BF16)

    _for_causal_prefix(i, t // tq, t, body)


def _mla_prompt(q_lat, q_rope, rows16, b, t, tq):
    nl, nr, row = MLA_HEADS * MLA_D_C, MLA_HEADS * MLA_D_ROPE, MLA_D_C + MLA_D_ROPE
    out = pl.pallas_call(
        _mla_prompt_kernel,
        grid=(b, t // tq),
        in_specs=[pl.BlockSpec((None, tq, nl), lambda bi, i: (bi, i, 0)),
                  pl.BlockSpec((None, tq, nr), lambda bi, i: (bi, i, 0)),
                  pl.BlockSpec((None, t, row), lambda bi, i: (bi, 0, 0))],
        out_specs=pl.BlockSpec((None, tq, nl), lambda bi, i: (bi, i, 0)),
        out_shape=jax.ShapeDtypeStruct((b, t, nl), BF16),
        compiler_params=_cparams("arbitrary", "arbitrary"),
        name="mla_prompt",
    )(q_lat.reshape(b, t, nl), q_rope.reshape(b, t, nr), rows16.reshape(b, t, row))
    return out.reshape(b * t, nl)


def _diff_lambda(lam_ref, lam_init):
    lam = lam_ref[...]
    a = jnp.sum(lam[0:1] * lam[1:2], axis=-1, keepdims=True)
    b = jnp.sum(lam[2:3] * lam[3:4], axis=-1, keepdims=True)
    return jnp.exp(a) - jnp.exp(b) + lam_init


def _diff_prompt_kernel(q_ref, kv_ref, lam_ref, g_ref, o_ref, *, lam_init):
    tq = q_ref.shape[0]
    t = kv_ref.shape[0]
    i = pl.program_id(1)
    scale = DIFF_D ** -0.5

    def body(kv_len):
        k12, v = kv_ref[0:kv_len, :2 * DIFF_D], kv_ref[0:kv_len, 2 * DIFF_D:]
        lam = _diff_lambda(lam_ref, lam_init)
        qpos = i * tq + lax.broadcasted_iota(jnp.int32, (tq, 1), 0)
        kpos = lax.broadcasted_iota(jnp.int32, (1, kv_len), 1)
        dist = (qpos - kpos).astype(F32)
        mask = dist >= 0
        low = lax.broadcasted_iota(jnp.int32, (tq, 2 * DIFF_D), 1) < DIFF_D
        zero = jnp.zeros((tq, 2 * DIFF_D), BF16)
        for h in range(DIFF_HEADS):
            bias = dist * (2.0 ** (-8.0 * (h + 1) / DIFF_HEADS))
            qh = q_ref[:, h * 2 * DIFF_D:(h + 1) * 2 * DIFF_D]
            outs = []
            for half in range(2):
                qm = jnp.where(low if half == 0 else jnp.logical_not(low), qh, zero)
                s = _dot_nt(qm, k12) * scale - bias
                p, inv = _softmax_parts(s, mask)
                outs.append(_dot(p.astype(BF16), v) * inv)
            o = outs[0] - lam * outs[1]
            o = _rms(o, g_ref[...]) * (1.0 - lam_init)
            o_ref[:, h * DIFF_D_V:(h + 1) * DIFF_D_V] = o.astype(BF16)

    _for_causal_prefix(i, t // tq, t, body)


def _diff_prompt(proj16, lam4, g_sub, lam_init, b, t, tq):
    nq = DIFF_HEADS * 2 * DIFF_D
    p3 = proj16.reshape(b, t, N_SMALL)
    out = pl.pallas_call(
        functools.partial(_diff_prompt_kernel, lam_init=lam_init),
        grid=(b, t // tq),
        in_specs=[pl.BlockSpec((None, tq, nq), lambda bi, i: (bi, i, C_DQ // nq)),
                  pl.BlockSpec((None, t, 256), lambda bi, i: (bi, 0, C_DKV // 256)),
                  pl.BlockSpec((4, DIFF_D), lambda bi, i: (0, 0)),
                  pl.BlockSpec((1, DIFF_D_V), lambda bi, i: (0, 0))],
        out_specs=pl.BlockSpec((None, tq, DIFF_HEADS * DIFF_D_V), lambda bi, i: (bi, i, 0)),
        out_shape=jax.ShapeDtypeStruct((b, t, DIFF_HEADS * DIFF_D_V), BF16),
        compiler_params=_cparams("arbitrary", "arbitrary"),
        name="diff_prompt",
    )(p3, p3, lam4, g_sub.reshape(1, -1))
    return out.reshape(b * t, -1)


def _compress_blocks(load_rows, pos_ref, wr_ref, n_blk):
    acc = jnp.zeros((n_blk, 2 * NSA_D), F32)
    for r in range(NSA_BLOCK):
        x = (load_rows(r) + pos_ref[r:r + 1, :]).astype(BF16)
        acc = acc + _dot(x, wr_ref[r])
    return acc


def _pad_q(q):
    return jnp.concatenate([q, jnp.zeros_like(q)], axis=1)


def _nsa_select(imp, cur, n_sel):
    n_blk = imp.shape[1]
    bid = lax.broadcasted_iota(jnp.int32, imp.shape, 1)
    forced = (bid == 0) | (bid == cur) | (bid == cur - 1)
    imp = jnp.where(forced, NSA_HEADS + 1.0, imp)
    imp = jnp.where(bid > cur, -1.0, imp)
    rank = jnp.zeros(imp.shape, F32)
    for j in range(n_blk):
        vj = imp[:, j:j + 1]
        beats = (vj > imp) | ((vj == imp) & (bid > j))
        rank = rank + jnp.where(beats, 1.0, 0.0)
    return rank < n_sel


def _nsa_prompt_kernel(q_ref, rows32_ref, rows16_ref, win_ref, tail_ref, pos_ref, wr_ref, exp_ref,
                       o_ref, kvc_ref, os_ref):
    tq = q_ref.shape[0]
    t = rows16_ref.shape[0]
    n_blk = kvc_ref.shape[0]
    i = pl.program_id(1)

    @pl.when(i == 0)
    def _():
        kvc_ref[...] = _compress_blocks(
            lambda r: rows32_ref[pl.ds(r, n_blk, stride=NSA_BLOCK), :], pos_ref, wr_ref, n_blk)

    kvc = kvc_ref[...].astype(BF16)
    start = pl.multiple_of(i * tq, tq)
    wkv = win_ref[pl.ds(start, NSA_WINDOW + tq), :]
    qpos = i * tq + lax.broadcasted_iota(jnp.int32, (tq, 1), 0)
    scale = NSA_D ** -0.5
    gate = _sigmoid(tail_ref[:, TAIL_NG:TAIL_NG + 3 * NSA_HEADS])

    blk_end = (lax.broadcasted_iota(jnp.int32, (1, n_blk), 1) + 1) * NSA_BLOCK - 1
    dist_c = (qpos - blk_end).astype(F32)
    mask_c = dist_c >= 0
    qs, o_cs = [], []
    imp = jnp.zeros((tq, n_blk), F32)
    for h in range(NSA_HEADS):
        qh = _pad_q(q_ref[:, h * NSA_D:(h + 1) * NSA_D])
        qs.append(qh)
        s = _dot_nt(qh, kvc) * scale - dist_c * (2.0 ** (-8.0 * (h + 1) / NSA_HEADS))
        p, inv = _softmax_parts(s, mask_c, guard_empty=True)
        p = p * inv
        imp = imp + p
        o_cs.append(_dot(p.astype(BF16), kvc)[:, NSA_D:])
    sel = _nsa_select(imp, qpos // NSA_BLOCK, min(NSA_TOP_N, n_blk))
    sel16 = jnp.where(sel, 1.0, 0.0).astype(BF16)

    def selected_branch(kv_len):
        nb = kv_len // NSA_BLOCK
        ksv = rows16_ref[0:kv_len, 2 * NSA_D:]
        sel_keys = _dot(sel16[:, :nb], exp_ref[0:nb, 0:kv_len]) > 0.5
        kpos = lax.broadcasted_iota(jnp.int32, (1, kv_len), 1)
        dist_s = (qpos - kpos).astype(F32)
        mask_s = sel_keys & (dist_s >= 0)
        for h in range(NSA_HEADS):
            slope = 2.0 ** (-8.0 * (h + 1) / NSA_HEADS)
            p, inv = _softmax_parts(_dot_nt(qs[h], ksv) * scale - dist_s * slope, mask_s)
            os_ref[:, h * NSA_D:(h + 1) * NSA_D] = (_dot(p.astype(BF16), ksv) * inv)[:, NSA_D:]

    _for_causal_prefix(i, t // tq, t, selected_branch)

    wpos = start - NSA_WINDOW + lax.broadcasted_iota(jnp.int32, (1, NSA_WINDOW + tq), 1)
    dw = qpos - wpos
    dist_w = dw.astype(F32)
    mask_w = (dw >= 0) & (dw < NSA_WINDOW) & (wpos >= 0)
    for h in range(NSA_HEADS):
        slope = 2.0 ** (-8.0 * (h + 1) / NSA_HEADS)
        p, inv = _softmax_parts(_dot_nt(qs[h], wkv) * scale - dist_w * slope, mask_w)
        o_w = (_dot(p.astype(BF16), wkv) * inv)[:, NSA_D:]
        o = (gate[:, 3 * h:3 * h + 1] * o_cs[h]
             + gate[:, 3 * h + 1:3 * h + 2] * os_ref[:, h * NSA_D:(h + 1) * NSA_D]
             + gate[:, 3 * h + 2:3 * h + 3] * o_w)
        o_ref[:, h * NSA_D:(h + 1) * NSA_D] = o.astype(BF16)


def _nsa_prompt(proj32, proj16, win_pad16, pos_r, w_r, expand, b, t, tq):
    nq = NSA_HEADS * NSA_D
    n_blk = t // NSA_BLOCK
    p32 = proj32.reshape(b, t, N_SMALL)
    p16 = proj16.reshape(b, t, N_SMALL)
    out = pl.pallas_call(
        _nsa_prompt_kernel,
        grid=(b, t // tq),
        in_specs=[pl.BlockSpec((None, tq, nq), lambda bi, i: (bi, i, C_NQ // nq)),
                  pl.BlockSpec((None, t, LANES), lambda bi, i: (bi, 0, C_NSA // LANES)),
                  pl.BlockSpec((None, t, 256), lambda bi, i: (bi, 0, C_NSA // 256)),
                  pl.BlockSpec((None, t + NSA_WINDOW, 2 * NSA_D), lambda bi, i: (bi, 0, 0)),
                  pl.BlockSpec((None, tq, LANES), lambda bi, i: (bi, i, C_TAIL // LANES)),
                  pl.BlockSpec(pos_r.shape, lambda bi, i: (0, 0)),
                  pl.BlockSpec(w_r.shape, lambda bi, i: (0, 0, 0)),
                  pl.BlockSpec(expand.shape, lambda bi, i: (0, 0))],
        out_specs=pl.BlockSpec((None, tq, nq), lambda bi, i: (bi, i, 0)),
        out_shape=jax.ShapeDtypeStruct((b, t, nq), BF16),
        scratch_shapes=[pltpu.VMEM((n_blk, 2 * NSA_D), F32), pltpu.VMEM((tq, nq), F32)],
        compiler_params=_cparams("arbitrary", "arbitrary"),
        name="nsa_prompt",
    )(p16, p32, p16, win_pad16, p32, pos_r, w_r, expand)
    return out.reshape(b * t, nq)


def _lane_page_copies(cache_ref, pt_ref, buf_ref, sem_ref, layer):
    rows = cache_ref.shape[3]

    def copies(seq, slot, page):
        dst = buf_ref.at[slot, :, pl.ds(pl.multiple_of(page * rows, rows), rows)]
        return [pltpu.make_async_copy(cache_ref.at[layer, pt_ref[seq, page]], dst, sem_ref.at[slot])]
    return copies


def _gather_step(copies, n_pages):
    b = pl.program_id(0)
    nb = pl.num_programs(0)
    slot = b % 2

    def start_all(seq, sl):
        def body(p, c):
            for cp in copies(seq, sl, p):
                cp.start()
            return c
        lax.fori_loop(0, n_pages, body, 0)

    @pl.when(b == 0)
    def _():
        start_all(0, 0)

    @pl.when(b + 1 < nb)
    def _():
        start_all(b + 1, 1 - slot)

    def wait_body(p, c):
        for cp in copies(b, slot, p):
            cp.wait()
        return c
    lax.fori_loop(0, n_pages, wait_body, 0)
    return slot


def _new_key_mask(n_rows, rows_per_tok, n_new_pad, n_new):
    tok = lax.broadcasted_iota(jnp.int32, (n_rows, n_new_pad), 0) // rows_per_tok
    j = lax.broadcasted_iota(jnp.int32, (n_rows, n_new_pad), 1)
    return (j <= tok) & (j < n_new), (tok - j).astype(F32)


def _two_part_softmax(s_past, mask_past, s_new, mask_new):
    if mask_past is not None:
        s_past = jnp.where(mask_past, s_past, NEG)
    s_new = jnp.where(mask_new, s_new, NEG)
    m = jnp.maximum(jnp.max(s_past, axis=-1, keepdims=True), jnp.max(s_new, axis=-1, keepdims=True))
    p_past = jnp.exp(s_past - m)
    p_new = jnp.exp(s_new - m)
    l = jnp.sum(p_past, axis=-1, keepdims=True) + jnp.sum(p_new, axis=-1, keepdims=True)
    return p_past, p_new, 1.0 / l


def _mla_decode_kernel(pt_ref, q_ref, new_ref, cache_ref, o_ref, buf_ref, sem_ref, *, layer, n_pages, n_new):
    slot = _gather_step(_lane_page_copies(cache_ref, pt_ref, buf_ref, sem_ref, layer), n_pages)
    kt = buf_ref[slot].astype(BF16)
    kn = new_ref[...]
    q = q_ref[...]
    scale = (MLA_D_NOPE + MLA_D_ROPE) ** -0.5
    mask_new, _ = _new_key_mask(q.shape[0], MLA_HEADS, kn.shape[0], n_new)
    p, pn, inv = _two_part_softmax(_dot(q, kt) * scale, None, _dot_nt(q, kn) * scale, mask_new)
    o = (_dot_nt(p.astype(BF16), kt[:MLA_D_C]) + _dot(pn.astype(BF16), kn[:, :MLA_D_C])) * inv
    o_ref[...] = o.astype(BF16)


def _decode_call(body, name, page_table, cache, layer, small_inputs, out_cols, scratch, extra_inputs=(),
                 layer_inputs=(), out_rows=None):
    db, n_pages = page_table.shape
    rows = out_rows or small_inputs[0].shape[1]
    in_specs = [pl.BlockSpec((None,) + a.shape[1:], lambda b, pt, nd=a.ndim: (b,) + (0,) * (nd - 1))
                for a in small_inputs]
    in_specs += [pl.BlockSpec((None, None) + a.shape[2:], lambda b, pt, nd=a.ndim: (layer, b) + (0,) * (nd - 2))
                 for a in layer_inputs]
    in_specs.append(pl.BlockSpec(memory_space=pl.ANY))
    in_specs += [pl.BlockSpec(a.shape, lambda b, pt, nd=a.ndim: (0,) * nd) for a in extra_inputs]
    grid_spec = pltpu.PrefetchScalarGridSpec(
        num_scalar_prefetch=1,
        grid=(db,),
        in_specs=in_specs,
        out_specs=pl.BlockSpec((None, rows, out_cols), lambda b, pt: (b, 0, 0)),
        scratch_shapes=list(scratch) + [pltpu.SemaphoreType.DMA((2,))],
    )
    return pl.pallas_call(
        functools.partial(body, layer=layer, n_pages=n_pages),
        grid_spec=grid_spec,
        out_shape=jax.ShapeDtypeStruct((db, rows, out_cols), BF16),
        compiler_params=_cparams("arbitrary"),
        name=name,
    )(page_table, *small_inputs, *layer_inputs, cache, *extra_inputs)


def _diff_decode_kernel(pt_ref, q_ref, new_ref, cache_ref, lam_ref, g_ref, o_ref, buf_ref, sem_ref,
                        *, layer, n_pages, n_new, lam_init, past_len):
    rows = cache_ref.shape[2]

    def copies(seq, sl, page):
        dst = pl.ds(pl.multiple_of(page * rows, rows), rows)
        return [pltpu.make_async_copy(cache_ref.at[layer, pt_ref[seq, page], :, c, :],
                                      buf_ref.at[sl, c, dst, :], sem_ref.at[sl]) for c in range(2)]
    slot = _gather_step(copies, n_pages)
    k12 = buf_ref[slot, 0].astype(BF16)
    v = buf_ref[slot, 1].astype(BF16)
    new = new_ref[...]
    kn, vn = new[:, :2 * DIFF_D], new[:, 2 * DIFF_D:]
    q = q_ref[...]
    n_rows = q.shape[0]
    n_half = n_rows // 2
    lk = k12.shape[0]
    r = lax.broadcasted_iota(jnp.int32, (n_rows, 1), 0) % n_half
    tok, head = r // DIFF_HEADS, r % DIFF_HEADS
    slope = jnp.exp2(-8.0 * (head + 1).astype(F32) / DIFF_HEADS)
    dist = (past_len + tok - lax.broadcasted_iota(jnp.int32, (1, lk), 1)).astype(F32)
    tokn = lax.broadcasted_iota(jnp.int32, (n_rows, new.shape[0]), 0) % n_half // DIFF_HEADS
    jn = lax.broadcasted_iota(jnp.int32, (n_rows, new.shape[0]), 1)
    mask_new = (jn <= tokn) & (jn < n_new)
    dist_new = (tokn - jn).astype(F32)
    scale = DIFF_D ** -0.5
    p, pn, inv = _two_part_softmax(_dot_nt(q, k12) * scale - slope * dist, None,
                                   _dot_nt(q, kn) * scale - slope * dist_new, mask_new)
    o = (_dot(p.astype(BF16), v) + _dot(pn.astype(BF16), vn)) * inv
    lam = _diff_lambda(lam_ref, lam_init)
    o = o[:n_half] - lam * o[n_half:]
    o_ref[...] = (_rms(o, g_ref[...]) * (1.0 - lam_init)).astype(BF16)


def _nsa_decode_kernel(pt_ref, q_ref, new_ref, gate_ref, win_ref, cache_ref, pos_ref, wr_ref, exp_ref,
                       o_ref, buf_ref, rows_ref, sem_ref, *, layer, n_pages, n_new, past_len):
    slot = _gather_step(_lane_page_copies(cache_ref, pt_ref, buf_ref, sem_ref, layer), n_pages)
    lk = buf_ref.shape[2]
    n_blk = lk // NSA_BLOCK
    q = _pad_q(q_ref[...])
    n_rows = q.shape[0]
    row = lax.broadcasted_iota(jnp.int32, (n_rows, 1), 0)
    tok, head = row // NSA_HEADS, row % NSA_HEADS
    qpos = past_len + tok
    slope = jnp.exp2(-8.0 * (head + 1).astype(F32) / NSA_HEADS)
    scale = NSA_D ** -0.5
    new = new_ref[...]
    n_new_pad = new.shape[0]
    tokn = lax.broadcasted_iota(jnp.int32, (n_rows, n_new_pad), 0) // NSA_HEADS
    jn = lax.broadcasted_iota(jnp.int32, (n_rows, n_new_pad), 1)
    mask_new = (jn <= tokn) & (jn < n_new)
    dist_new = (tokn - jn).astype(F32)

    for c in range(lk // LANES):
        rows_ref[c * LANES:(c + 1) * LANES, :] = buf_ref[slot, 0:2 * NSA_D, c * LANES:(c + 1) * LANES].T
    kvc = _compress_blocks(lambda r: rows_ref[pl.ds(r, n_blk, stride=NSA_BLOCK), :],
                           pos_ref, wr_ref, n_blk).astype(BF16)
    blk_end = (lax.broadcasted_iota(jnp.int32, (1, n_blk), 1) + 1) * NSA_BLOCK - 1
    dist_c = (qpos - blk_end).astype(F32)
    p, inv = _softmax_parts(_dot_nt(q, kvc) * scale - slope * dist_c, dist_c >= 0, guard_empty=True)
    p = p * inv
    o_c = _dot(p.astype(BF16), kvc)[:, NSA_D:]
    n_tok = n_rows // NSA_HEADS
    imp = jnp.sum(p.reshape(n_tok, NSA_HEADS, n_blk), axis=1)

    bid = lax.broadcasted_iota(jnp.int32, (1, n_blk), 1)
    forced = (bid == 0) | (bid == n_blk - 1)
    val = jnp.where(forced, NSA_HEADS + 1.0, imp)
    pad_rows = -n_tok % 8
    val_t = jnp.concatenate([val, jnp.zeros((pad_rows, n_blk), F32)], axis=0).T if pad_rows else val.T
    bi = lax.broadcasted_iota(jnp.int32, (n_blk, n_blk), 0)
    bj = lax.broadcasted_iota(jnp.int32, (n_blk, n_blk), 1)
    n_sel = min(NSA_TOP_N, n_blk + 1)
    sel_rows = []
    for tk in range(n_tok):
        vi = val_t[:, tk:tk + 1]
        vj = val[tk:tk + 1, :]
        beats = (vi > vj) | ((vi == vj) & (bi < bj))
        rank = jnp.sum(jnp.where(beats, 1.0, 0.0), axis=0, keepdims=True)
        rank = rank + jnp.where(vj < NSA_HEADS + 1.0, 1.0, 0.0)
        sel_t = jnp.where(rank < n_sel, 1.0, 0.0)
        sel_rows.append(jnp.broadcast_to(sel_t, (NSA_HEADS, n_blk)))
    sel = jnp.concatenate(sel_rows, axis=0).astype(BF16)
    sel_keys = _dot(sel, exp_ref[...]) > 0.5

    q64 = q_ref[...]
    k_sel_t = buf_ref[slot, 2 * NSA_D:3 * NSA_D, :].astype(BF16)
    v_sel_t = buf_ref[slot, 3 * NSA_D:4 * NSA_D, :].astype(BF16)
    k_sel_n, v_sel_n = new[:, 2 * NSA_D:3 * NSA_D], new[:, 3 * NSA_D:4 * NSA_D]
    dist_s = (qpos - lax.broadcasted_iota(jnp.int32, (1, lk), 1)).astype(F32)
    p, pn, inv = _two_part_softmax(_dot(q64, k_sel_t) * scale - slope * dist_s, sel_keys,
                                   _dot_nt(q64, k_sel_n) * scale - slope * dist_new, mask_new)
    o_s = (_dot_nt(p.astype(BF16), v_sel_t) + _dot(pn.astype(BF16), v_sel_n)) * inv

    win = win_ref[...].astype(BF16)
    kw_t, vw_t = win[:NSA_D], win[NSA_D:]
    lb = win.shape[1]
    kw_n, vw_n = new[:, 4 * NSA_D:5 * NSA_D], new[:, 5 * NSA_D:]
    wpos = past_len - lb + lax.broadcasted_iota(jnp.int32, (1, lb), 1)
    dw = qpos - wpos
    mask_w = (dw >= 0) & (dw < NSA_WINDOW)
    p, pn, inv = _two_part_softmax(_dot(q64, kw_t) * scale - slope * dw.astype(F32), mask_w,
                                   _dot_nt(q64, kw_n) * scale - slope * dist_new, mask_new)
    o_w = (_dot_nt(p.astype(BF16), vw_t) + _dot(pn.astype(BF16), vw_n)) * inv

    gate = _sigmoid(gate_ref[...])
    o = gate[:, 0:1] * o_c + gate[:, 1:2] * o_s + gate[:, 2:3] * o_w
    o_ref[...] = o.astype(BF16)


def _branch_kernel(o0_ref, o1_ref, o2_ref, g_ref, w_ref, out_ref):
    d = out_ref.shape[1]
    acc = None
    for n, o_ref in enumerate((o0_ref, o1_ref, o2_ref)):
        term = g_ref[:, n * d:(n + 1) * d].astype(F32) * _dot(o_ref[...], w_ref[n])
        acc = term if acc is None else acc + term
    out_ref[...] = acc.astype(BF16)


def _branch_merge(o_mla, o_diff, o_nsa, gates16, w_branch16, tm):
    m = o_mla.shape[0]
    d = w_branch16.shape[2]
    o_spec = pl.BlockSpec((tm, BRANCH_W), lambda i: (i, 0))
    return pl.pallas_call(
        _branch_kernel,
        grid=(m // tm,),
        in_specs=[o_spec, o_spec, o_spec,
                  pl.BlockSpec((tm, N_BRANCH * d), lambda i: (i, 0)),
                  pl.BlockSpec(w_branch16.shape, lambda i: (0, 0, 0))],
        out_specs=pl.BlockSpec((tm, d), lambda i: (i, 0)),
        out_shape=jax.ShapeDtypeStruct((m, d), BF16),
        compiler_params=_cparams("arbitrary"),
        name="branch_merge",
    )(o_mla, o_diff, o_nsa, gates16, w_branch16)


def _ffn_kernel(*refs, moe):
    if moe:
        h_ref, wg_ref, wu_ref, wd_ref, x_ref, gate_ref, eg_ref, o_ref, acc_ref = refs
    else:
        h_ref, wg_ref, wu_ref, wd_ref, x_ref, gate_ref, o_ref, acc_ref = refs
    j = pl.program_id(1)

    @pl.when(j == 0)
    def _():
        acc_ref[...] = jnp.zeros_like(acc_ref)

    h = h_ref[...]
    g = _dot(h, wg_ref[...])
    a = g * _sigmoid(g) * _dot(h, wu_ref[...])
    if moe:
        eg = eg_ref[...]
        a = a * jnp.concatenate([eg] * (a.shape[1] // LANES), axis=1)
    acc_ref[...] += _dot(a.astype(BF16), wd_ref[...])

    @pl.when(j == pl.num_programs(1) - 1)
    def _():
        o_ref[...] = x_ref[...] + gate_ref[...] * acc_ref[...]


def _ffn(h, wg, wu, wd, x, st, c_gate, tf, expert_gates=None):
    m, d, tm = st.m, st.d, st.tm
    ne, _, f = wg.shape
    per_e = f // tf
    moe = expert_gates is not None
    in_specs = [pl.BlockSpec((tm, d), lambda i, j: (i, 0)),
                pl.BlockSpec((None, d, tf), lambda i, j: (j // per_e, 0, j % per_e)),
                pl.BlockSpec((None, d, tf), lambda i, j: (j // per_e, 0, j % per_e)),
                pl.BlockSpec((None, tf, d), lambda i, j: (j // per_e, j % per_e, 0)),
                pl.BlockSpec((tm, d), lambda i, j: (i, 0)),
                st.mod_spec(c_gate)]
    args = [h, wg, wu, wd, x, st.mod3]
    if moe:
        in_specs.append(pl.BlockSpec((None, tm, LANES), lambda i, j: (j // per_e, i, 0)))
        args.append(expert_gates)
    return pl.pallas_call(
        functools.partial(_ffn_kernel, moe=moe),
        grid=(m // tm, ne * per_e),
        in_specs=in_specs,
        out_specs=pl.BlockSpec((tm, d), lambda i, j: (i, 0)),
        out_shape=jax.ShapeDtypeStruct((m, d), F32),
        scratch_shapes=[pltpu.VMEM((tm, d), F32)],
        compiler_params=_cparams("arbitrary", "arbitrary"),
        name="ffn_moe" if moe else "ffn_dense",
    )(*args)


def _rot_cols(w):
    half = w.shape[-1] // 2
    return jnp.concatenate([-w[..., half:], w[..., :half]], axis=-1)


def _prep_w_in(w):
    d = w.shape[0]
    sizes = (MLA_D_Q, MLA_D_C, MLA_D_ROPE, DIFF_HEADS * 2 * DIFF_D, 2 * DIFF_D, DIFF_D_V,
             NSA_HEADS * NSA_D, 6 * NSA_D, 3 * NSA_HEADS, N_BRANCH * d)
    parts, start = [], 0
    for s in sizes:
        parts.append(w[:, start:start + s])
        start += s
    cq, ckv, kr, dq, dk, dv, nq, nkv, ng, mg = parts
    pad = jnp.zeros((d, N_SMALL - C_TAIL - 2 * MLA_D_ROPE - 3 * NSA_HEADS), w.dtype)
    small = jnp.concatenate([dq, nq, cq, ckv, dk, dv, nkv, kr, _rot_cols(kr), ng, pad], axis=1)
    return small.astype(BF16), mg.astype(BF16)


def _prep_w_uq(w_uq):
    w = w_uq.reshape(MLA_D_Q, MLA_HEADS, MLA_D_NOPE + MLA_D_ROPE)
    nope = w[:, :, :MLA_D_NOPE].reshape(MLA_D_Q, -1)
    rope = w[:, :, MLA_D_NOPE:]
    return jnp.concatenate([nope, rope.reshape(MLA_D_Q, -1), _rot_cols(rope).reshape(MLA_D_Q, -1)],
                           axis=1).astype(BF16)


def _block_diag(blocks):
    h, r, c = blocks.shape
    eye = jnp.eye(h, dtype=blocks.dtype)
    return (eye[:, None, :, None] * blocks[:, :, None, :]).reshape(h * r, h * c)


def _rope_tables(pos, reps):
    half = MLA_D_ROPE // 2
    freqs = ROPE_BASE ** (-jnp.arange(half, dtype=F32) / half)
    ang = pos.astype(F32)[:, None] * freqs
    cos = jnp.concatenate([jnp.cos(ang), jnp.cos(ang)], axis=1)
    sin = jnp.concatenate([jnp.sin(ang), jnp.sin(ang)], axis=1)
    return jnp.tile(cos, (1, reps)), jnp.tile(sin, (1, reps))


def _prep_cmp(w_cmp, pos_cmp):
    wk = w_cmp[0].reshape(NSA_BLOCK, NSA_D, NSA_D)
    wv = w_cmp[1].reshape(NSA_BLOCK, NSA_D, NSA_D)
    z = jnp.zeros_like(wk)
    w_r = jnp.concatenate([jnp.concatenate([wk, z], axis=2), jnp.concatenate([z, wv], axis=2)], axis=1)
    pos_r = jnp.concatenate([pos_cmp[0], pos_cmp[1]], axis=1)
    return w_r.astype(BF16), pos_r


def _expand_matrix(n_blk, n_keys):
    return (jnp.arange(n_keys)[None, :] // NSA_BLOCK == jnp.arange(n_blk)[:, None]).astype(BF16)


def _pad_rows(a, rows):
    return jnp.pad(a, ((0, 0), (0, rows - a.shape[1]), (0, 0)))


def kernel(x_prompt, x_sample, cache_mla, cache_diff, cache_nsa, state_nsa_win, page_table, c_prompt, c_sample, w_mod, b_mod, g_attn, g_ffn, w_in, mla_g_q, mla_w_uq, mla_g_kv, mla_w_uk, mla_w_uv, diff_lambda, diff_g_sub, nsa_w_cmp, nsa_pos_cmp, w_branch, w_out, w_gate_dense, w_up_dense, w_down_dense, w_router, b_router, w_gate_exp, w_up_exp, w_down_exp, g_final):
    b, t, d = x_prompt.shape
    db, dt, _ = x_sample.shape
    depth = w_mod.shape[0]
    n_pages, page = page_table.shape[1], cache_mla.shape[2]
    past_len = n_pages * page
    mp, ms = b * t, db * dt
    tm_p, tm_s = _tile(t, 512), _tile(ms, 256)
    tq = _tile(t, 256)
    new_pad = 8

    rows_c = -(-(b + db) // 8) * 8
    c_all = jnp.pad(jnp.concatenate([c_prompt, c_sample], axis=0), ((0, rows_c - b - db), (0, 0)))
    mod = _adaln(c_all, w_mod, b_mod)

    cos_p, sin_p = _rope_tables(jnp.arange(t), MLA_HEADS)
    cos_s, sin_s = _rope_tables(past_len + jnp.arange(dt), MLA_HEADS)
    cos_s, sin_s = jnp.tile(cos_s, (db, 1)), jnp.tile(sin_s, (db, 1))
    exp_p = _expand_matrix(t // NSA_BLOCK, t)
    exp_s = _expand_matrix(past_len // NSA_BLOCK, past_len)
    tf = 512
    cache_mla_t = jnp.swapaxes(cache_mla, 2, 3)
    cache_nsa_t = jnp.transpose(cache_nsa, (0, 1, 3, 4, 2)).reshape(depth, -1, 4 * NSA_D, page)
    win_t = jnp.transpose(state_nsa_win, (0, 1, 3, 4, 2)).reshape(depth, db, 2 * NSA_D, -1)

    xp, xs = x_prompt.reshape(mp, d), x_sample.reshape(ms, d)
    outs = {k: [] for k in ("mla_p", "mla_s", "diff_p", "diff_s", "nsa_p", "nsa_s", "win_p", "win_s")}
    for i in range(depth):
        lam_init = 0.8 - 0.6 * math.exp(-0.3 * i)
        w_small, w_mg = _prep_w_in(w_in[i])
        w_uq_r = _prep_w_uq(mla_w_uq[i])
        w_uk_bd = _block_diag(jnp.transpose(mla_w_uk[i], (1, 2, 0))).astype(BF16)
        w_uv_bd = _block_diag(jnp.transpose(mla_w_uv[i], (1, 0, 2))).astype(BF16)
        w_r, pos_r = _prep_cmp(nsa_w_cmp[i], nsa_pos_cmp[i])
        w_branch16, w_out16 = w_branch[i].astype(BF16), w_out[i].astype(BF16)
        st_p = _Stream(mp, tm_p, mod[i, :b].reshape(b, 1, 6 * d), t // tm_p, d)
        st_s = _Stream(ms, tm_s, jnp.repeat(mod[i, b:b + db], dt, axis=0).reshape(ms // tm_s, tm_s, 6 * d), 1, d)

        new_x = []
        for x, st, is_prompt in ((xp, st_p, True), (xs, st_s, False)):
            tm = st.tm
            h = _norm_mod(x, g_attn[i], st, 1, 0)
            proj32, proj16 = _matmul(h, w_small, tm, N_SMALL, mode="dual")
            gates16 = _matmul(h, w_mg, tm, 1024, mode="sigmoid")
            q_nope, q_rope, mla_rows, mla_rows16 = _mla_prep(
                proj32, mla_g_q[i], mla_g_kv[i], w_uq_r,
                cos_p if is_prompt else cos_s, sin_p if is_prompt else sin_s, tm)
            q_lat = _matmul(q_nope, w_uk_bd, tm, w_uk_bd.shape[1])
            diff_rows = proj32[:, C_DKV:C_DKV + 2 * DIFF_D_V]
            nsa_rows = proj32[:, C_NSA:C_NSA + 4 * NSA_D]
            win_rows = proj32[:, C_WIN:C_WIN + 2 * NSA_D]
            if is_prompt:
                o_lat = _mla_prompt(q_lat, q_rope, mla_rows16, b, t, tq)
                o_diff = _diff_prompt(proj16, diff_lambda[i], diff_g_sub[i], lam_init, b, t, tq)
                win16 = proj16[:, C_WIN:C_WIN + 2 * NSA_D].reshape(b, t, 2 * NSA_D)
                win_pad16 = jnp.pad(win16, ((0, 0), (NSA_WINDOW, 0), (0, 0)))
                o_nsa = _nsa_prompt(proj32, proj16, win_pad16, pos_r, w_r, exp_p, b, t, tq)
                outs["mla_p"].append(mla_rows.reshape(b, t, -1))
                outs["diff_p"].append(diff_rows.reshape(b, t, 2, DIFF_D_V))
                outs["nsa_p"].append(nsa_rows.reshape(b, t, 4, NSA_D))
                win = win_rows.reshape(b, t, 2, NSA_D)
                outs["win_p"].append(win[:, t - min(NSA_WINDOW, t):])
            else:
                nh = MLA_HEADS
                q_mla = jnp.concatenate([q_lat.reshape(ms, nh, MLA_D_C), q_rope.reshape(ms, nh, MLA_D_ROPE)],
                                        axis=2).reshape(db, dt * nh, MLA_D_C + MLA_D_ROPE)
                new_mla = _pad_rows(mla_rows16.reshape(db, dt, -1), new_pad)
                o_lat = _decode_call(
                    functools.partial(_mla_decode_kernel, n_new=dt), "mla_decode", page_table, cache_mla_t, i,
                    [q_mla, new_mla], MLA_D_C,
                    [pltpu.VMEM((2, cache_mla_t.shape[2], past_len), F32)]).reshape(ms, nh * MLA_D_C)
                dq = proj16[:, C_DQ:C_DQ + DIFF_HEADS * 2 * DIFF_D].reshape(db, dt * DIFF_HEADS, 2, DIFF_D)
                zq = jnp.zeros_like(dq[:, :, 0])
                q_diff = jnp.concatenate([jnp.concatenate([dq[:, :, 0], zq], axis=2),
                                          jnp.concatenate([zq, dq[:, :, 1]], axis=2)], axis=1)
                new_diff = _pad_rows(proj16[:, C_DKV:C_DKV + 2 * DIFF_D_V].reshape(db, dt, -1), new_pad)
                o_diff = _decode_call(
                    functools.partial(_diff_decode_kernel, n_new=dt, lam_init=lam_init, past_len=past_len),
                    "diff_decode", page_table, cache_diff, i, [q_diff, new_diff], DIFF_D_V,
                    [pltpu.VMEM((2, 2, past_len, DIFF_D_V), F32)],
                    extra_inputs=[diff_lambda[i], diff_g_sub[i].reshape(1, -1)],
                    out_rows=dt * DIFF_HEADS).reshape(ms, DIFF_HEADS * DIFF_D_V)
                q_nsa = proj16[:, C_NQ:C_NQ + NSA_HEADS * NSA_D].reshape(db, dt * NSA_HEADS, NSA_D)
                new_nsa = _pad_rows(proj16[:, C_NSA:C_NSA + 6 * NSA_D].reshape(db, dt, -1), new_pad)
                ng = proj32[:, C_TAIL + TAIL_NG:C_TAIL + TAIL_NG + 3 * NSA_HEADS].reshape(db, dt * NSA_HEADS, 3)
                o_nsa = _decode_call(
                    functools.partial(_nsa_decode_kernel, n_new=dt, past_len=past_len),
                    "nsa_decode", page_table, cache_nsa_t, i, [q_nsa, new_nsa, ng], NSA_D,
                    [pltpu.VMEM((2, 4 * NSA_D, past_len), F32), pltpu.VMEM((past_len, 2 * NSA_D), F32)],
                    layer_inputs=[win_t], extra_inputs=[pos_r, w_r, exp_s]).reshape(ms, NSA_HEADS * NSA_D)
                outs["mla_s"].append(mla_rows.reshape(db, dt, -1))
                outs["diff_s"].append(diff_rows.reshape(db, dt, 2, DIFF_D_V))
                outs["nsa_s"].append(nsa_rows.reshape(db, dt, 4, NSA_D))
                win_all = jnp.concatenate([state_nsa_win[i], win_rows.reshape(db, dt, 2, NSA_D)], axis=1)
                outs["win_s"].append(win_all[:, win_all.shape[1] - min(NSA_WINDOW, win_all.shape[1]):])
            o_mla = _matmul(o_lat, w_uv_bd, tm, w_uv_bd.shape[1])
            merged = _branch_merge(o_mla, o_diff, o_nsa, gates16, w_branch16, tm)
            x = _matmul(merged, w_out16, tm, d, mode="residual", res=x, st=st, c_gate=2)
            j = i // 2
            if i % 2 == 0:
                hf = _norm_mod(x, g_ffn[i], st, 4, 3)
                x = _ffn(hf, w_gate_dense[j].astype(BF16)[None], w_up_dense[j].astype(BF16)[None],
                         w_down_dense[j].astype(BF16)[None], x, st, 5, tf)
            else:
                hf, eg = _norm_mod_route(x, g_ffn[i], st, 4, 3, w_router[j], b_router[j])
                fe = w_gate_exp.shape[3]
                fe_pad = -(-fe // tf) * tf
                wg = jnp.pad(w_gate_exp[j], ((0, 0), (0, 0), (0, fe_pad - fe))).astype(BF16)
                wu = jnp.pad(w_up_exp[j], ((0, 0), (0, 0), (0, fe_pad - fe))).astype(BF16)
                wd = jnp.pad(w_down_exp[j], ((0, 0), (0, fe_pad - fe), (0, 0))).astype(BF16)
                x = _ffn(hf, wg, wu, wd, x, st, 5, tf, expert_gates=eg)
            new_x.append(x)
        xp, xs = new_x

    y_p = _final_norm(xp, g_final, tm_p).reshape(b, t, d)
    y_s = _final_norm(xs, g_final, tm_s).reshape(db, dt, d)
    return (y_p, y_s,
            jnp.stack(outs["mla_p"]), jnp.stack(outs["mla_s"]),
            jnp.stack(outs["diff_p"]), jnp.stack(outs["diff_s"]),
            jnp.stack(outs["nsa_p"]), jnp.stack(outs["nsa_s"]),
            jnp.stack(outs["win_p"]), jnp.stack(outs["win_s"]))
```

```python
import functools
import math

import jax
import jax.numpy as jnp
from jax import lax
from jax.experimental import pallas as pl
from jax.experimental.pallas import tpu as pltpu

F32 = jnp.float32
BF16 = jnp.bfloat16

MLA_HEADS, MLA_D_NOPE, MLA_D_ROPE, MLA_D_V, MLA_D_C, MLA_D_Q = 8, 64, 32, 64, 128, 384
ROPE_BASE = 10000.0
DIFF_HEADS, DIFF_D = 4, 64
DIFF_D_V = 2 * DIFF_D
NSA_HEADS, NSA_D, NSA_BLOCK, NSA_TOP_N, NSA_WINDOW = 8, 64, 64, 16, 512
N_BRANCH, BRANCH_W = 3, 512
N_EXPERTS, TOP_K = 8, 2
EPS = 1e-6
NEG = -1e30

LANES = 128
V7X_VMEM_LIMIT = 56 * 1024 * 1024

C_DQ, C_NQ, C_CQ, C_CKV, C_DKV, C_NSA, C_WIN, C_TAIL, N_SMALL = 0, 512, 1024, 1408, 1536, 1792, 2048, 2176, 2304
TAIL_KR, TAIL_KROT, TAIL_NG = 0, 32, 64


def _cparams(*sem):
    return pltpu.CompilerParams(dimension_semantics=sem, vmem_limit_bytes=V7X_VMEM_LIMIT)


def _tile(n, pref):
    if n <= pref:
        return n
    t = pref - pref % 8
    while t > 8 and n % t:
        t -= 8
    assert n % t == 0, (n, pref)
    return t


def _sigmoid(x):
    return 1.0 / (1.0 + jnp.exp(-x))


def _dot(a, b):
    return jnp.dot(a, b, preferred_element_type=F32)


def _dot_nt(a, b):
    return lax.dot_general(a, b, (((1,), (1,)), ((), ())), preferred_element_type=F32)


def _rms(x, g):
    return x * lax.rsqrt(jnp.mean(x * x, axis=-1, keepdims=True) + EPS) * g


def _softmax_parts(s, mask, guard_empty=False):
    s = jnp.where(mask, s, NEG)
    m = jnp.max(s, axis=-1, keepdims=True)
    p = jnp.exp(s - m)
    if guard_empty:
        p = jnp.where(mask, p, 0.0)
    l = jnp.sum(p, axis=-1, keepdims=True)
    if guard_empty:
        l = jnp.maximum(l, 1e-30)
    return p, 1.0 / l


def _adaln_kernel(c_ref, w_ref, b_ref, o_ref):
    c = c_ref[...]
    a = (c * _sigmoid(c)).astype(BF16)
    o_ref[...] = _dot(a, w_ref[...].astype(BF16)) + b_ref[...]


def _adaln(c_all, w_mod, b_mod):
    depth, d, n = w_mod.shape
    rows = c_all.shape[0]
    tn = _tile(n, 1024)
    return pl.pallas_call(
        _adaln_kernel,
        grid=(depth, n // tn),
        in_specs=[pl.BlockSpec((rows, d), lambda l, j: (0, 0)),
                  pl.BlockSpec((None, d, tn), lambda l, j: (l, 0, j)),
                  pl.BlockSpec((None, 1, tn), lambda l, j: (l, 0, j))],
        out_specs=pl.BlockSpec((None, rows, tn), lambda l, j: (l, 0, j)),
        out_shape=jax.ShapeDtypeStruct((depth, rows, n), F32),
        compiler_params=_cparams("arbitrary", "arbitrary"),
        name="adaln",
    )(c_all, w_mod, b_mod.reshape(depth, 1, n))


class _Stream:
    def __init__(self, m, tm, mod3, per, d):
        self.m, self.tm, self.mod3, self.per, self.d = m, tm, mod3, per, d

    def mod_spec(self, chunk, row_axis=0):
        r, d, per = self.mod3.shape[1], self.d, self.per
        if row_axis == 0:
            return pl.BlockSpec((None, r, d), lambda i, *_: (i // per, 0, chunk))
        return pl.BlockSpec((None, r, d), lambda j, i: (i // per, 0, chunk))


def _norm_mod_kernel(x_ref, g_ref, sc_ref, sh_ref, o_ref):
    y = _rms(x_ref[...], g_ref[...])
    o_ref[...] = (y * (1.0 + sc_ref[...]) + sh_ref[...]).astype(o_ref.dtype)


def _norm_mod(x, g, st, c_scale, c_shift):
    m, d, tm = st.m, st.d, st.tm
    return pl.pallas_call(
        _norm_mod_kernel,
        grid=(m // tm,),
        in_specs=[pl.BlockSpec((tm, d), lambda i: (i, 0)),
                  pl.BlockSpec((1, d), lambda i: (0, 0)),
                  st.mod_spec(c_scale), st.mod_spec(c_shift)],
        out_specs=pl.BlockSpec((tm, d), lambda i: (i, 0)),
        out_shape=jax.ShapeDtypeStruct((m, d), BF16),
        compiler_params=_cparams("arbitrary"),
        name="norm_mod",
    )(x, g.reshape(1, d), st.mod3, st.mod3)


def _norm_mod_route_kernel(x_ref, g_ref, sc_ref, sh_ref, wr_ref, br_ref, o_ref, gate_ref):
    y = _rms(x_ref[...], g_ref[...])
    h = y * (1.0 + sc_ref[...]) + sh_ref[...]
    o_ref[...] = h.astype(o_ref.dtype)
    logits = jnp.dot(h, wr_ref[...], preferred_element_type=F32,
                     precision=lax.Precision.HIGHEST) + br_ref[...]
    ne = logits.shape[-1]
    ids = lax.broadcasted_iota(jnp.int32, logits.shape, 1)
    m1 = jnp.max(logits, axis=-1, keepdims=True)
    i1 = jnp.min(jnp.where(logits == m1, ids, ne), axis=-1, keepdims=True)
    rest = jnp.where(ids == i1, -jnp.inf, logits)
    m2 = jnp.max(rest, axis=-1, keepdims=True)
    i2 = jnp.min(jnp.where(rest == m2, ids, ne), axis=-1, keepdims=True)
    e = jnp.exp(m2 - m1)
    w1 = 1.0 / (1.0 + e)
    w2 = e * w1
    gates = jnp.where(ids == i1, w1, jnp.where(ids == i2, w2, 0.0))
    for k in range(ne):
        gate_ref[k] = jnp.broadcast_to(gates[:, k:k + 1], gate_ref.shape[1:])


def _norm_mod_route(x, g, st, c_scale, c_shift, w_router, b_router):
    m, d, tm = st.m, st.d, st.tm
    ne = w_router.shape[1]
    return pl.pallas_call(
        _norm_mod_route_kernel,
        grid=(m // tm,),
        in_specs=[pl.BlockSpec((tm, d), lambda i: (i, 0)),
                  pl.BlockSpec((1, d), lambda i: (0, 0)),
                  st.mod_spec(c_scale), st.mod_spec(c_shift),
                  pl.BlockSpec((d, ne), lambda i: (0, 0)),
                  pl.BlockSpec((1, ne), lambda i: (0, 0))],
        out_specs=[pl.BlockSpec((tm, d), lambda i: (i, 0)),
                   pl.BlockSpec((ne, tm, LANES), lambda i: (0, i, 0))],
        out_shape=[jax.ShapeDtypeStruct((m, d), BF16),
                   jax.ShapeDtypeStruct((ne, m, LANES), F32)],
        compiler_params=_cparams("arbitrary"),
        name="norm_mod_route",
    )(x, g.reshape(1, d), st.mod3, st.mod3, w_router, b_router.reshape(1, ne))


def _final_norm_kernel(x_ref, g_ref, o_ref):
    o_ref[...] = _rms(x_ref[...], g_ref[...])


def _final_norm(x, g, tm):
    m, d = x.shape
    return pl.pallas_call(
        _final_norm_kernel,
        grid=(m // tm,),
        in_specs=[pl.BlockSpec((tm, d), lambda i: (i, 0)), pl.BlockSpec((1, d), lambda i: (0, 0))],
        out_specs=pl.BlockSpec((tm, d), lambda i: (i, 0)),
        out_shape=jax.ShapeDtypeStruct((m, d), F32),
        compiler_params=_cparams("arbitrary"),
        name="final_norm",
    )(x, g.reshape(1, d))


def _mm_plain_kernel(x_ref, w_ref, o_ref):
    o_ref[...] = _dot(x_ref[...], w_ref[...]).astype(o_ref.dtype)


def _in_proj_kernel(x_ref, w_ref, o32_ref, o16_ref, diff_ref, nsa_ref, win_ref):
    acc = _dot(x_ref[...], w_ref[...])
    o32_ref[...] = acc
    o16_ref[...] = acc.astype(BF16)
    diff_ref[...] = acc[:, C_DKV:C_DKV + 2 * DIFF_D_V]
    nsa_ref[...] = acc[:, C_NSA:C_NSA + 4 * NSA_D]
    win_ref[...] = acc[:, C_WIN:C_WIN + 2 * NSA_D]


def _in_proj(x, w, tm):
    m, k = x.shape
    widths = (N_SMALL, N_SMALL, 2 * DIFF_D_V, 4 * NSA_D, 2 * NSA_D)
    dtypes = (F32, BF16, F32, F32, F32)
    return pl.pallas_call(
        _in_proj_kernel,
        grid=(m // tm,),
        in_specs=[pl.BlockSpec((tm, k), lambda i: (i, 0)), pl.BlockSpec((k, N_SMALL), lambda i: (0, 0))],
        out_specs=[pl.BlockSpec((tm, n), lambda i: (i, 0)) for n in widths],
        out_shape=[jax.ShapeDtypeStruct((m, n), dt) for n, dt in zip(widths, dtypes)],
        compiler_params=_cparams("arbitrary"),
        name="in_proj",
    )(x, w)


def _mm_sigmoid_kernel(x_ref, w_ref, o_ref):
    o_ref[...] = _sigmoid(_dot(x_ref[...], w_ref[...])).astype(o_ref.dtype)


def _mm_residual_kernel(x_ref, w_ref, res_ref, gate_ref, o_ref):
    o_ref[...] = res_ref[...] + gate_ref[...] * _dot(x_ref[...], w_ref[...])


def _matmul(x, w, tm, tn, mode="plain", out_dtype=BF16, res=None, st=None, c_gate=None):
    m, k = x.shape
    n = w.shape[1]
    tn = _tile(n, tn)
    in_specs = [pl.BlockSpec((tm, k), lambda j, i: (i, 0)),
                pl.BlockSpec((k, tn), lambda j, i: (0, j))]
    o_spec = pl.BlockSpec((tm, tn), lambda j, i: (i, j))
    args = [x, w]
    if mode == "plain":
        body, out_specs, out_shape = _mm_plain_kernel, o_spec, jax.ShapeDtypeStruct((m, n), out_dtype)
    elif mode == "sigmoid":
        body, out_specs, out_shape = _mm_sigmoid_kernel, o_spec, jax.ShapeDtypeStruct((m, n), out_dtype)
    else:
        assert tn == n == st.d
        body, out_specs, out_shape = _mm_residual_kernel, o_spec, jax.ShapeDtypeStruct((m, n), F32)
        in_specs += [pl.BlockSpec((tm, tn), lambda j, i: (i, j)), st.mod_spec(c_gate, row_axis=1)]
        args += [res, st.mod3]
    return pl.pallas_call(
        body, grid=(n // tn, m // tm), in_specs=in_specs, out_specs=out_specs, out_shape=out_shape,
        compiler_params=_cparams("arbitrary", "arbitrary"), name="mm_" + mode,
    )(*args)


def _mla_prep_kernel(cqkv_ref, tail_ref, gq_ref, gkv_ref, wuq_ref, cos_ref, sin_ref,
                     qn_ref, qr_ref, rows_ref, rows16_ref):
    cqkv = cqkv_ref[...]
    cq, ckv = cqkv[:, :MLA_D_Q], cqkv[:, MLA_D_Q:]
    cos, sin = cos_ref[...], sin_ref[...]
    q = _dot(_rms(cq, gq_ref[...]).astype(BF16), wuq_ref[...])
    n_nope = MLA_HEADS * MLA_D_NOPE
    n_rope = MLA_HEADS * MLA_D_ROPE
    qn_ref[...] = q[:, :n_nope].astype(BF16)
    qr_ref[...] = (q[:, n_nope:n_nope + n_rope] * cos + q[:, n_nope + n_rope:] * sin).astype(BF16)
    tail = tail_ref[...]
    kr = (tail[:, TAIL_KR:TAIL_KR + MLA_D_ROPE] * cos[:, :MLA_D_ROPE]
          + tail[:, TAIL_KROT:TAIL_KROT + MLA_D_ROPE] * sin[:, :MLA_D_ROPE])
    ckvn = _rms(ckv, gkv_ref[...])
    rows_ref[:, :MLA_D_C] = ckvn
    rows_ref[:, MLA_D_C:] = kr
    rows16_ref[:, :MLA_D_C] = ckvn.astype(BF16)
    rows16_ref[:, MLA_D_C:] = kr.astype(BF16)


def _mla_prep(proj32, g_q, g_kv, w_uq_r, cos_t, sin_t, tm):
    m = proj32.shape[0]
    n_pos = cos_t.shape[0] // tm
    n_nope, n_rope = MLA_HEADS * MLA_D_NOPE, MLA_HEADS * MLA_D_ROPE
    row = MLA_D_C + MLA_D_ROPE
    return pl.pallas_call(
        _mla_prep_kernel,
        grid=(m // tm,),
        in_specs=[pl.BlockSpec((tm, 512), lambda i: (i, C_CQ // 512)),
                  pl.BlockSpec((tm, LANES), lambda i: (i, C_TAIL // LANES)),
                  pl.BlockSpec((1, MLA_D_Q), lambda i: (0, 0)),
                  pl.BlockSpec((1, MLA_D_C), lambda i: (0, 0)),
                  pl.BlockSpec(w_uq_r.shape, lambda i: (0, 0)),
                  pl.BlockSpec((tm, n_rope), lambda i: (i % n_pos, 0)),
                  pl.BlockSpec((tm, n_rope), lambda i: (i % n_pos, 0))],
        out_specs=[pl.BlockSpec((tm, n_nope), lambda i: (i, 0)),
                   pl.BlockSpec((tm, n_rope), lambda i: (i, 0)),
                   pl.BlockSpec((tm, row), lambda i: (i, 0)),
                   pl.BlockSpec((tm, row), lambda i: (i, 0))],
        out_shape=[jax.ShapeDtypeStruct((m, n_nope), BF16),
                   jax.ShapeDtypeStruct((m, n_rope), BF16),
                   jax.ShapeDtypeStruct((m, row), F32),
                   jax.ShapeDtypeStruct((m, row), BF16)],
        compiler_params=_cparams("arbitrary"),
        name="mla_prep",
    )(proj32, proj32, g_q.reshape(1, -1), g_kv.reshape(1, -1), w_uq_r, cos_t, sin_t)


CAUSAL_CLASSES = 4


def _for_causal_prefix(i, n_q_blocks, t, body):
    n_cls = CAUSAL_CLASSES if n_q_blocks % CAUSAL_CLASSES == 0 else 1
    cls = i * n_cls // n_q_blocks
    for c in range(n_cls):
        pl.when(cls == c)(functools.partial(body, t * (c + 1) // n_cls))


def _mla_prompt_kernel(ql_ref, qr_ref, k_ref, o_ref):
    tq = ql_ref.shape[0]
    t = k_ref.shape[0]
    i = pl.program_id(1)
    scale = (MLA_D_NOPE + MLA_D_ROPE) ** -0.5

    def body(kv_len):
        kc, kr = k_ref[0:kv_len, :MLA_D_C], k_ref[0:kv_len, MLA_D_C:]
        qpos = i * tq + lax.broadcasted_iota(jnp.int32, (tq, 1), 0)
        kpos = lax.broadcasted_iota(jnp.int32, (1, kv_len), 1)
        mask = kpos <= qpos
        for h in range(MLA_HEADS):
            ql = ql_ref[:, h * MLA_D_C:(h + 1) * MLA_D_C]
            qr = qr_ref[:, h * MLA_D_ROPE:(h + 1) * MLA_D_ROPE]
            s = (_dot_nt(ql, kc) + _dot_nt(qr, kr)) * scale
            p, inv = _softmax_parts(s, mask)
            o_ref[:, h * MLA_D_C:(h + 1) * MLA_D_C] = (_dot(p.astype(BF16), kc) * inv).astype(BF16)

    _for_causal_prefix(i, t // tq, t, body)


def _mla_prompt(q_lat, q_rope, rows16, b, t, tq):
    nl, nr, row = MLA_HEADS * MLA_D_C, MLA_HEADS * MLA_D_ROPE, MLA_D_C + MLA_D_ROPE
    out = pl.pallas_call(
        _mla_prompt_kernel,
        grid=(b, t // tq),
        in_specs=[pl.BlockSpec((None, tq, nl), lambda bi, i: (bi, i, 0)),
                  pl.BlockSpec((None, tq, nr), lambda bi, i: (bi, i, 0)),
                  pl.BlockSpec((None, t, row), lambda bi, i: (bi, 0, 0))],
        out_specs=pl.BlockSpec((None, tq, nl), lambda bi, i: (bi, i, 0)),
        out_shape=jax.ShapeDtypeStruct((b, t, nl), BF16),
        compiler_params=_cparams("arbitrary", "arbitrary"),
        name="mla_prompt",
    )(q_lat.reshape(b, t, nl), q_rope.reshape(b, t, nr), rows16.reshape(b, t, row))
    return out.reshape(b * t, nl)


def _diff_lambda(lam_ref, lam_init):
    lam = lam_ref[...]
    a = jnp.sum(lam[0:1] * lam[1:2], axis=-1, keepdims=True)
    b = jnp.sum(lam[2:3] * lam[3:4], axis=-1, keepdims=True)
    return jnp.exp(a) - jnp.exp(b) + lam_init


def _diff_prompt_kernel(q_ref, kv_ref, lam_ref, g_ref, o_ref, *, lam_init):
    tq = q_ref.shape[0]
    t = kv_ref.shape[0]
    i = pl.program_id(1)
    scale16 = jnp.asarray(DIFF_D ** -0.5, BF16)

    def body(kv_len):
        k12, v = kv_ref[0:kv_len, :2 * DIFF_D], kv_ref[0:kv_len, 2 * DIFF_D:]
        lam = _diff_lambda(lam_ref, lam_init)
        qpos = i * tq + lax.broadcasted_iota(jnp.int32, (tq, 1), 0)
        kpos = lax.broadcasted_iota(jnp.int32, (1, kv_len), 1)
        mask = kpos <= qpos
        kposf = kpos.astype(F32)
        low = lax.broadcasted_iota(jnp.int32, (tq, 2 * DIFF_D), 1) < DIFF_D
        zero = jnp.zeros((tq, 2 * DIFF_D), BF16)
        for h in range(DIFF_HEADS):
            bias = kposf * (2.0 ** (-8.0 * (h + 1) / DIFF_HEADS))
            qh = q_ref[:, h * 2 * DIFF_D:(h + 1) * 2 * DIFF_D] * scale16
            outs = []
            for half in range(2):
                qm = jnp.where(low if half == 0 else jnp.logical_not(low), qh, zero)
                s = _dot_nt(qm, k12) + bias
                p, inv = _softmax_parts(s, mask)
                outs.append(_dot(p.astype(BF16), v) * inv)
            o = outs[0] - lam * outs[1]
            o = _rms(o, g_ref[...]) * (1.0 - lam_init)
            o_ref[:, h * DIFF_D_V:(h + 1) * DIFF_D_V] = o.astype(BF16)

    _for_causal_prefix(i, t // tq, t, body)


def _diff_prompt(proj16, lam4, g_sub, lam_init, b, t, tq):
    nq = DIFF_HEADS * 2 * DIFF_D
    p3 = proj16.reshape(b, t, N_SMALL)
    out = pl.pallas_call(
        functools.partial(_diff_prompt_kernel, lam_init=lam_init),
        grid=(b, t // tq),
        in_specs=[pl.BlockSpec((None, tq, nq), lambda bi, i: (bi, i, C_DQ // nq)),
                  pl.BlockSpec((None, t, 256), lambda bi, i: (bi, 0, C_DKV // 256)),
                  pl.BlockSpec((4, DIFF_D), lambda bi, i: (0, 0)),
                  pl.BlockSpec((1, DIFF_D_V), lambda bi, i: (0, 0))],
        out_specs=pl.BlockSpec((None, tq, DIFF_HEADS * DIFF_D_V), lambda bi, i: (bi, i, 0)),
        out_shape=jax.ShapeDtypeStruct((b, t, DIFF_HEADS * DIFF_D_V), BF16),
        compiler_params=_cparams("arbitrary", "arbitrary"),
        name="diff_prompt",
    )(p3, p3, lam4, g_sub.reshape(1, -1))
    return out.reshape(b * t, -1)


def _compress_blocks(load_rows, pos_ref, wr_ref, n_blk):
    acc = jnp.zeros((n_blk, 2 * NSA_D), F32)
    for r in range(NSA_BLOCK):
        x = (load_rows(r) + pos_ref[r:r + 1, :]).astype(BF16)
        acc = acc + _dot(x, wr_ref[r])
    return acc


def _pad_q(q):
    return jnp.concatenate([q, jnp.zeros_like(q)], axis=1)


def _nsa_select(imp, cur, n_sel):
    n_blk = imp.shape[1]
    bid = lax.broadcasted_iota(jnp.int32, imp.shape, 1)
    forced = (bid == 0) | (bid == cur) | (bid == cur - 1)
    imp = jnp.where(forced, NSA_HEADS + 1.0, imp)
    imp = jnp.where(bid > cur, -1.0, imp)
    rank = jnp.zeros(imp.shape, F32)
    for j in range(n_blk):
        vj = imp[:, j:j + 1]
        beats = (vj > imp) | ((vj == imp) & (bid > j))
        rank = rank + jnp.where(beats, 1.0, 0.0)
    return rank < n_sel


def _nsa_prompt_kernel(q_ref, rows32_ref, rows16_ref, win_ref, tail_ref, pos_ref, wr_ref, exp_ref,
                       o_ref, kvc_ref, os_ref):
    tq = q_ref.shape[0]
    t = rows16_ref.shape[0]
    n_blk = kvc_ref.shape[0]
    i = pl.program_id(1)

    @pl.when(i == 0)
    def _():
        kvc_ref[...] = _compress_blocks(
            lambda r: rows32_ref[pl.ds(r, n_blk, stride=NSA_BLOCK), :], pos_ref, wr_ref, n_blk)

    kvc = kvc_ref[...].astype(BF16)
    start = pl.multiple_of(i * tq, tq)
    wkv = win_ref[pl.ds(start, NSA_WINDOW + tq), :]
    qpos = i * tq + lax.broadcasted_iota(jnp.int32, (tq, 1), 0)
    scale16 = jnp.asarray(NSA_D ** -0.5, BF16)
    gate = _sigmoid(tail_ref[:, TAIL_NG:TAIL_NG + 3 * NSA_HEADS])

    blk_end = (lax.broadcasted_iota(jnp.int32, (1, n_blk), 1) + 1) * NSA_BLOCK - 1
    mask_c = blk_end <= qpos
    blk_endf = blk_end.astype(F32)
    qs, o_cs = [], []
    imp = jnp.zeros((tq, n_blk), F32)
    for h in range(NSA_HEADS):
        qh = _pad_q(q_ref[:, h * NSA_D:(h + 1) * NSA_D] * scale16)
        qs.append(qh)
        s = _dot_nt(qh, kvc) + blk_endf * (2.0 ** (-8.0 * (h + 1) / NSA_HEADS))
        p, inv = _softmax_parts(s, mask_c, guard_empty=True)
        p = p * inv
        imp = imp + p
        o_cs.append(_dot(p.astype(BF16), kvc)[:, NSA_D:])
    sel = _nsa_select(imp, qpos // NSA_BLOCK, min(NSA_TOP_N, n_blk))
    sel16 = jnp.where(sel, 1.0, 0.0).astype(BF16)

    def selected_branch(kv_len):
        nb = kv_len // NSA_BLOCK
        ksv = rows16_ref[0:kv_len, 2 * NSA_D:]
        sel_keys = _dot(sel16[:, :nb], exp_ref[0:nb, 0:kv_len]) > 0.5
        kpos = lax.broadcasted_iota(jnp.int32, (1, kv_len), 1)
        mask_s = sel_keys & (kpos <= qpos)
        kposf = kpos.astype(F32)
        for h in range(NSA_HEADS):
            slope = 2.0 ** (-8.0 * (h + 1) / NSA_HEADS)
            p, inv = _softmax_parts(_dot_nt(qs[h], ksv) + kposf * slope, mask_s)
            os_ref[:, h * NSA_D:(h + 1) * NSA_D] = (_dot(p.astype(BF16), ksv) * inv)[:, NSA_D:]

    _for_causal_prefix(i, t // tq, t, selected_branch)

    wpos = start - NSA_WINDOW + lax.broadcasted_iota(jnp.int32, (1, NSA_WINDOW + tq), 1)
    dw = qpos - wpos
    mask_w = (dw >= 0) & (dw < NSA_WINDOW) & (wpos >= 0)
    wposf = (wpos - start).astype(F32)
    for h in range(NSA_HEADS):
        slope = 2.0 ** (-8.0 * (h + 1) / NSA_HEADS)
        p, inv = _softmax_parts(_dot_nt(qs[h], wkv) + wposf * slope, mask_w)
        o_w = (_dot(p.astype(BF16), wkv) * inv)[:, NSA_D:]
        o = (gate[:, 3 * h:3 * h + 1] * o_cs[h]
             + gate[:, 3 * h + 1:3 * h + 2] * os_ref[:, h * NSA_D:(h + 1) * NSA_D]
             + gate[:, 3 * h + 2:3 * h + 3] * o_w)
        o_ref[:, h * NSA_D:(h + 1) * NSA_D] = o.astype(BF16)


def _nsa_prompt(proj32, proj16, win_pad16, pos_r, w_r, expand, b, t, tq):
    nq = NSA_HEADS * NSA_D
    n_blk = t // NSA_BLOCK
    p32 = proj32.reshape(b, t, N_SMALL)
    p16 = proj16.reshape(b, t, N_SMALL)
    out = pl.pallas_call(
        _nsa_prompt_kernel,
        grid=(b, t // tq),
        in_specs=[pl.BlockSpec((None, tq, nq), lambda bi, i: (bi, i, C_NQ // nq)),
                  pl.BlockSpec((None, t, LANES), lambda bi, i: (bi, 0, C_NSA // LANES)),
                  pl.BlockSpec((None, t, 256), lambda bi, i: (bi, 0, C_NSA // 256)),
                  pl.BlockSpec((None, t + NSA_WINDOW, 2 * NSA_D), lambda bi, i: (bi, 0, 0)),
                  pl.BlockSpec((None, tq, LANES), lambda bi, i: (bi, i, C_TAIL // LANES)),
                  pl.BlockSpec(pos_r.shape, lambda bi, i: (0, 0)),
                  pl.BlockSpec(w_r.shape, lambda bi, i: (0, 0, 0)),
                  pl.BlockSpec(expand.shape, lambda bi, i: (0, 0))],
        out_specs=pl.BlockSpec((None, tq, nq), lambda bi, i: (bi, i, 0)),
        out_shape=jax.ShapeDtypeStruct((b, t, nq), BF16),
        scratch_shapes=[pltpu.VMEM((n_blk, 2 * NSA_D), F32), pltpu.VMEM((tq, nq), F32)],
        compiler_params=_cparams("arbitrary", "arbitrary"),
        name="nsa_prompt",
    )(p16, p32, p16, win_pad16, p32, pos_r, w_r, expand)
    return out.reshape(b * t, nq)


def _lane_page_copies(cache_ref, pt_ref, buf_ref, sem_ref, layer):
    rows = cache_ref.shape[3]

    def copies(seq, slot, page):
        dst = buf_ref.at[slot, :, pl.ds(pl.multiple_of(page * rows, rows), rows)]
        return [pltpu.make_async_copy(cache_ref.at[layer, pt_ref[seq, page]], dst, sem_ref.at[slot])]
    return copies


def _gather_step(copies, n_pages):
    b = pl.program_id(0)
    nb = pl.num_programs(0)
    slot = b % 2

    def start_all(seq, sl):
        def body(p, c):
            for cp in copies(seq, sl, p):
                cp.start()
            return c
        lax.fori_loop(0, n_pages, body, 0)

    @pl.when(b == 0)
    def _():
        start_all(0, 0)

    @pl.when(b + 1 < nb)
    def _():
        start_all(b + 1, 1 - slot)

    def wait_body(p, c):
        for cp in copies(b, slot, p):
            cp.wait()
        return c
    lax.fori_loop(0, n_pages, wait_body, 0)
    return slot


def _new_key_mask(n_rows, rows_per_tok, n_new_pad, n_new):
    tok = lax.broadcasted_iota(jnp.int32, (n_rows, n_new_pad), 0) // rows_per_tok
    j = lax.broadcasted_iota(jnp.int32, (n_rows, n_new_pad), 1)
    return (j <= tok) & (j < n_new), (tok - j).astype(F32)


def _two_part_softmax(s_past, mask_past, s_new, mask_new):
    if mask_past is not None:
        s_past = jnp.where(mask_past, s_past, NEG)
    s_new = jnp.where(mask_new, s_new, NEG)
    m = jnp.maximum(jnp.max(s_past, axis=-1, keepdims=True), jnp.max(s_new, axis=-1, keepdims=True))
    p_past = jnp.exp(s_past - m)
    p_new = jnp.exp(s_new - m)
    l = jnp.sum(p_past, axis=-1, keepdims=True) + jnp.sum(p_new, axis=-1, keepdims=True)
    return p_past, p_new, 1.0 / l


def _mla_decode_kernel(pt_ref, q_ref, new_ref, cache_ref, o_ref, buf_ref, sem_ref, *, layer, n_pages, n_new):
    slot = _gather_step(_lane_page_copies(cache_ref, pt_ref, buf_ref, sem_ref, layer), n_pages)
    kt = buf_ref[slot].astype(BF16)
    kn = new_ref[...]
    q = q_ref[...]
    scale = (MLA_D_NOPE + MLA_D_ROPE) ** -0.5
    mask_new, _ = _new_key_mask(q.shape[0], MLA_HEADS, kn.shape[0], n_new)
    p, pn, inv = _two_part_softmax(_dot(q, kt) * scale, None, _dot_nt(q, kn) * scale, mask_new)
    o = (_dot_nt(p.astype(BF16), kt[:MLA_D_C]) + _dot(pn.astype(BF16), kn[:, :MLA_D_C])) * inv
    o_ref[...] = o.astype(BF16)


def _decode_call(body, name, page_table, cache, layer, small_inputs, out_cols, scratch, extra_inputs=(),
                 layer_inputs=(), out_rows=None):
    db, n_pages = page_table.shape
    rows = out_rows or small_inputs[0].shape[1]
    in_specs = [pl.BlockSpec((None,) + a.shape[1:], lambda b, pt, nd=a.ndim: (b,) + (0,) * (nd - 1))
                for a in small_inputs]
    in_specs += [pl.BlockSpec((None, None) + a.shape[2:], lambda b, pt, nd=a.ndim: (layer, b) + (0,) * (nd - 2))
                 for a in layer_inputs]
    in_specs.append(pl.BlockSpec(memory_space=pl.ANY))
    in_specs += [pl.BlockSpec(a.shape, lambda b, pt, nd=a.ndim: (0,) * nd) for a in extra_inputs]
    grid_spec = pltpu.PrefetchScalarGridSpec(
        num_scalar_prefetch=1,
        grid=(db,),
        in_specs=in_specs,
        out_specs=pl.BlockSpec((None, rows, out_cols), lambda b, pt: (b, 0, 0)),
        scratch_shapes=list(scratch) + [pltpu.SemaphoreType.DMA((2,))],
    )
    return pl.pallas_call(
        functools.partial(body, layer=layer, n_pages=n_pages),
        grid_spec=grid_spec,
        out_shape=jax.ShapeDtypeStruct((db, rows, out_cols), BF16),
        compiler_params=_cparams("arbitrary"),
        name=name,
    )(page_table, *small_inputs, *layer_inputs, cache, *extra_inputs)


def _diff_decode_kernel(pt_ref, q_ref, new_ref, cache_ref, lam_ref, g_ref, o_ref, buf_ref, sem_ref,
                        *, layer, n_pages, n_new, lam_init, past_len):
    rows = cache_ref.shape[2]

    def copies(seq, sl, page):
        dst = pl.ds(pl.multiple_of(page * rows, rows), rows)
        return [pltpu.make_async_copy(cache_ref.at[layer, pt_ref[seq, page], :, c, :],
                                      buf_ref.at[sl, c, dst, :], sem_ref.at[sl]) for c in range(2)]
    slot = _gather_step(copies, n_pages)
    k12 = buf_ref[slot, 0].astype(BF16)
    v = buf_ref[slot, 1].astype(BF16)
    new = new_ref[...]
    kn, vn = new[:, :2 * DIFF_D], new[:, 2 * DIFF_D:]
    q = q_ref[...]
    n_rows = q.shape[0]
    n_half = n_rows // 2
    lk = k12.shape[0]
    r = lax.broadcasted_iota(jnp.int32, (n_rows, 1), 0) % n_half
    tok, head = r // DIFF_HEADS, r % DIFF_HEADS
    slope = jnp.exp2(-8.0 * (head + 1).astype(F32) / DIFF_HEADS)
    dist = (past_len + tok - lax.broadcasted_iota(jnp.int32, (1, lk), 1)).astype(F32)
    tokn = lax.broadcasted_iota(jnp.int32, (n_rows, new.shape[0]), 0) % n_half // DIFF_HEADS
    jn = lax.broadcasted_iota(jnp.int32, (n_rows, new.shape[0]), 1)
    mask_new = (jn <= tokn) & (jn < n_new)
    dist_new = (tokn - jn).astype(F32)
    scale = DIFF_D ** -0.5
    p, pn, inv = _two_part_softmax(_dot_nt(q, k12) * scale - slope * dist, None,
                                   _dot_nt(q, kn) * scale - slope * dist_new, mask_new)
    o = (_dot(p.astype(BF16), v) + _dot(pn.astype(BF16), vn)) * inv
    lam = _diff_lambda(lam_ref, lam_init)
    o = o[:n_half] - lam * o[n_half:]
    o_ref[...] = (_rms(o, g_ref[...]) * (1.0 - lam_init)).astype(BF16)


def _nsa_decode_kernel(pt_ref, q_ref, new_ref, gate_ref, win_ref, cache_ref, pos_ref, wr_ref, exp_ref,
                       o_ref, buf_ref, rows_ref, sem_ref, *, layer, n_pages, n_new, past_len):
    slot = _gather_step(_lane_page_copies(cache_ref, pt_ref, buf_ref, sem_ref, layer), n_pages)
    lk = buf_ref.shape[2]
    n_blk = lk // NSA_BLOCK
    q = _pad_q(q_ref[...])
    n_rows = q.shape[0]
    row = lax.broadcasted_iota(jnp.int32, (n_rows, 1), 0)
    tok, head = row // NSA_HEADS, row % NSA_HEADS
    qpos = past_len + tok
    slope = jnp.exp2(-8.0 * (head + 1).astype(F32) / NSA_HEADS)
    scale = NSA_D ** -0.5
    new = new_ref[...]
    n_new_pad = new.shape[0]
    tokn = lax.broadcasted_iota(jnp.int32, (n_rows, n_new_pad), 0) // NSA_HEADS
    jn = lax.broadcasted_iota(jnp.int32, (n_rows, n_new_pad), 1)
    mask_new = (jn <= tokn) & (jn < n_new)
    dist_new = (tokn - jn).astype(F32)

    tiles = [buf_ref[slot, 0:2 * NSA_D, c * LANES:(c + 1) * LANES].T for c in range(lk // LANES)]
    x3 = jnp.concatenate(tiles, axis=0).reshape(n_blk, NSA_BLOCK, 2 * NSA_D)
    xr = jnp.swapaxes(x3, 0, 1)
    kvc = _compress_blocks(lambda r: xr[r], pos_ref, wr_ref, n_blk).astype(BF16)
    blk_end = (lax.broadcasted_iota(jnp.int32, (1, n_blk), 1) + 1) * NSA_BLOCK - 1
    dist_c = (qpos - blk_end).astype(F32)
    p, inv = _softmax_parts(_dot_nt(q, kvc) * scale - slope * dist_c, dist_c >= 0, guard_empty=True)
    p = p * inv
    o_c = _dot(p.astype(BF16), kvc)[:, NSA_D:]
    n_tok = n_rows // NSA_HEADS
    imp = jnp.sum(p.reshape(n_tok, NSA_HEADS, n_blk), axis=1)

    bid = lax.broadcasted_iota(jnp.int32, (1, n_blk), 1)
    forced = (bid == 0) | (bid == n_blk - 1)
    val = jnp.where(forced, NSA_HEADS + 1.0, imp)
    pad_rows = -n_tok % 8
    val_t = jnp.concatenate([val, jnp.zeros((pad_rows, n_blk), F32)], axis=0).T if pad_rows else val.T
    bi = lax.broadcasted_iota(jnp.int32, (n_blk, n_blk), 0)
    bj = lax.broadcasted_iota(jnp.int32, (n_blk, n_blk), 1)
    n_sel = min(NSA_TOP_N, n_blk + 1)
    sel_rows = []
    for tk in range(n_tok):
        vi = val_t[:, tk:tk + 1]
        vj = val[tk:tk + 1, :]
        beats = (vi > vj) | ((vi == vj) & (bi < bj))
        rank = jnp.sum(jnp.where(beats, 1.0, 0.0), axis=0, keepdims=True)
        rank = rank + jnp.where(vj < NSA_HEADS + 1.0, 1.0, 0.0)
        sel_t = jnp.where(rank < n_sel, 1.0, 0.0)
        sel_rows.append(jnp.broadcast_to(sel_t, (NSA_HEADS, n_blk)))
    sel = jnp.concatenate(sel_rows, axis=0).astype(BF16)
    sel_keys = _dot(sel, exp_ref[...]) > 0.5

    q64 = q_ref[...]
    k_sel_t = buf_ref[slot, 2 * NSA_D:3 * NSA_D, :].astype(BF16)
    v_sel_t = buf_ref[slot, 3 * NSA_D:4 * NSA_D, :].astype(BF16)
    k_sel_n, v_sel_n = new[:, 2 * NSA_D:3 * NSA_D], new[:, 3 * NSA_D:4 * NSA_D]
    dist_s = (qpos - lax.broadcasted_iota(jnp.int32, (1, lk), 1)).astype(F32)
    p, pn, inv = _two_part_softmax(_dot(q64, k_sel_t) * scale - slope * dist_s, sel_keys,
                                   _dot_nt(q64, k_sel_n) * scale - slope * dist_new, mask_new)
    o_s = (_dot_nt(p.astype(BF16), v_sel_t) + _dot(pn.astype(BF16), v_sel_n)) * inv

    win = win_ref[...].astype(BF16)
    kw_t, vw_t = win[:NSA_D], win[NSA_D:]
    lb = win.shape[1]
    kw_n, vw_n = new[:, 4 * NSA_D:5 * NSA_D], new[:, 5 * NSA_D:]
    wpos = past_len - lb + lax.broadcasted_iota(jnp.int32, (1, lb), 1)
    dw = qpos - wpos
    mask_w = (dw >= 0) & (dw < NSA_WINDOW)
    p, pn, inv = _two_part_softmax(_dot(q64, kw_t) * scale - slope * dw.astype(F32), mask_w,
                                   _dot_nt(q64, kw_n) * scale - slope * dist_new, mask_new)
    o_w = (_dot_nt(p.astype(BF16), vw_t) + _dot(pn.astype(BF16), vw_n)) * inv

    gate = _sigmoid(gate_ref[...])
    o = gate[:, 0:1] * o_c + gate[:, 1:2] * o_s + gate[:, 2:3] * o_w
    o_ref[...] = o.astype(BF16)


def _branch_kernel(o0_ref, o1_ref, o2_ref, g_ref, w_ref, out_ref):
    d = out_ref.shape[1]
    acc = None
    for n, o_ref in enumerate((o0_ref, o1_ref, o2_ref)):
        term = g_ref[:, n * d:(n + 1) * d].astype(F32) * _dot(o_ref[...], w_ref[n])
        acc = term if acc is None else acc + term
    out_ref[...] = acc.astype(BF16)


def _branch_merge(o_mla, o_diff, o_nsa, gates16, w_branch16, tm):
    m = o_mla.shape[0]
    d = w_branch16.shape[2]
    o_spec = pl.BlockSpec((tm, BRANCH_W), lambda i: (i, 0))
    return pl.pallas_call(
        _branch_kernel,
        grid=(m // tm,),
        in_specs=[o_spec, o_spec, o_spec,
                  pl.BlockSpec((tm, N_BRANCH * d), lambda i: (i, 0)),
                  pl.BlockSpec(w_branch16.shape, lambda i: (0, 0, 0))],
        out_specs=pl.BlockSpec((tm, d), lambda i: (i, 0)),
        out_shape=jax.ShapeDtypeStruct((m, d), BF16),
        compiler_params=_cparams("arbitrary"),
        name="branch_merge",
    )(o_mla, o_diff, o_nsa, gates16, w_branch16)


def _ffn_kernel(*refs, moe):
    if moe:
        h_ref, wg_ref, wu_ref, wd_ref, x_ref, gate_ref, eg_ref, o_ref, acc_ref = refs
    else:
        h_ref, wg_ref, wu_ref, wd_ref, x_ref, gate_ref, o_ref, acc_ref = refs
    j = pl.program_id(1)

    @pl.when(j == 0)
    def _():
        acc_ref[...] = jnp.zeros_like(acc_ref)

    h = h_ref[...]
    g = _dot(h, wg_ref[...])
    a = g * _sigmoid(g) * _dot(h, wu_ref[...])
    if moe:
        eg = eg_ref[...]
        a = a * jnp.concatenate([eg] * (a.shape[1] // LANES), axis=1)
    acc_ref[...] += _dot(a.astype(BF16), wd_ref[...])

    @pl.when(j == pl.num_programs(1) - 1)
    def _():
        o_ref[...] = x_ref[...] + gate_ref[...] * acc_ref[...]


def _ffn(h, wg, wu, wd, x, st, c_gate, tf, expert_gates=None):
    m, d, tm = st.m, st.d, st.tm
    ne, _, f = wg.shape
    per_e = f // tf
    moe = expert_gates is not None
    in_specs = [pl.BlockSpec((tm, d), lambda i, j: (i, 0)),
                pl.BlockSpec((None, d, tf), lambda i, j: (j // per_e, 0, j % per_e)),
                pl.BlockSpec((None, d, tf), lambda i, j: (j // per_e, 0, j % per_e)),
                pl.BlockSpec((None, tf, d), lambda i, j: (j // per_e, j % per_e, 0)),
                pl.BlockSpec((tm, d), lambda i, j: (i, 0)),
                st.mod_spec(c_gate)]
    args = [h, wg, wu, wd, x, st.mod3]
    if moe:
        in_specs.append(pl.BlockSpec((None, tm, LANES), lambda i, j: (j // per_e, i, 0)))
        args.append(expert_gates)
    return pl.pallas_call(
        functools.partial(_ffn_kernel, moe=moe),
        grid=(m // tm, ne * per_e),
        in_specs=in_specs,
        out_specs=pl.BlockSpec((tm, d), lambda i, j: (i, 0)),
        out_shape=jax.ShapeDtypeStruct((m, d), F32),
        scratch_shapes=[pltpu.VMEM((tm, d), F32)],
        compiler_params=_cparams("arbitrary", "arbitrary"),
        name="ffn_moe" if moe else "ffn_dense",
    )(*args)


def _rot_cols(w):
    half = w.shape[-1] // 2
    return jnp.concatenate([-w[..., half:], w[..., :half]], axis=-1)


def _prep_w_in(w):
    d = w.shape[0]
    sizes = (MLA_D_Q, MLA_D_C, MLA_D_ROPE, DIFF_HEADS * 2 * DIFF_D, 2 * DIFF_D, DIFF_D_V,
             NSA_HEADS * NSA_D, 6 * NSA_D, 3 * NSA_HEADS, N_BRANCH * d)
    parts, start = [], 0
    for s in sizes:
        parts.append(w[:, start:start + s])
        start += s
    cq, ckv, kr, dq, dk, dv, nq, nkv, ng, mg = parts
    pad = jnp.zeros((d, N_SMALL - C_TAIL - 2 * MLA_D_ROPE - 3 * NSA_HEADS), w.dtype)
    small = jnp.concatenate([dq, nq, cq, ckv, dk, dv, nkv, kr, _rot_cols(kr), ng, pad], axis=1)
    return small.astype(BF16), mg.astype(BF16)


def _prep_w_uq(w_uq):
    w = w_uq.reshape(MLA_D_Q, MLA_HEADS, MLA_D_NOPE + MLA_D_ROPE)
    nope = w[:, :, :MLA_D_NOPE].reshape(MLA_D_Q, -1)
    rope = w[:, :, MLA_D_NOPE:]
    return jnp.concatenate([nope, rope.reshape(MLA_D_Q, -1), _rot_cols(rope).reshape(MLA_D_Q, -1)],
                           axis=1).astype(BF16)


def _block_diag(blocks):
    h, r, c = blocks.shape
    eye = jnp.eye(h, dtype=blocks.dtype)
    return (eye[:, None, :, None] * blocks[:, :, None, :]).reshape(h * r, h * c)


def _rope_tables(pos, reps):
    half = MLA_D_ROPE // 2
    freqs = ROPE_BASE ** (-jnp.arange(half, dtype=F32) / half)
    ang = pos.astype(F32)[:, None] * freqs
    cos = jnp.concatenate([jnp.cos(ang), jnp.cos(ang)], axis=1)
    sin = jnp.concatenate([jnp.sin(ang), jnp.sin(ang)], axis=1)
    return jnp.tile(cos, (1, reps)), jnp.tile(sin, (1, reps))


def _prep_cmp(w_cmp, pos_cmp):
    wk = w_cmp[0].reshape(NSA_BLOCK, NSA_D, NSA_D)
    wv = w_cmp[1].reshape(NSA_BLOCK, NSA_D, NSA_D)
    z = jnp.zeros_like(wk)
    w_r = jnp.concatenate([jnp.concatenate([wk, z], axis=2), jnp.concatenate([z, wv], axis=2)], axis=1)
    pos_r = jnp.concatenate([pos_cmp[0], pos_cmp[1]], axis=1)
    return w_r.astype(BF16), pos_r


def _expand_matrix(n_blk, n_keys):
    return (jnp.arange(n_keys)[None, :] // NSA_BLOCK == jnp.arange(n_blk)[:, None]).astype(BF16)


def _pad_rows(a, rows):
    return jnp.pad(a, ((0, 0), (0, rows - a.shape[1]), (0, 0)))


def kernel(x_prompt, x_sample, cache_mla, cache_diff, cache_nsa, state_nsa_win, page_table, c_prompt, c_sample, w_mod, b_mod, g_attn, g_ffn, w_in, mla_g_q, mla_w_uq, mla_g_kv, mla_w_uk, mla_w_uv, diff_lambda, diff_g_sub, nsa_w_cmp, nsa_pos_cmp, w_branch, w_out, w_gate_dense, w_up_dense, w_down_dense, w_router, b_router, w_gate_exp, w_up_exp, w_down_exp, g_final):
    b, t, d = x_prompt.shape
    db, dt, _ = x_sample.shape
    depth = w_mod.shape[0]
    n_pages, page = page_table.shape[1], cache_mla.shape[2]
    past_len = n_pages * page
    mp, ms = b * t, db * dt
    tm_p, tm_s = _tile(t, 512), _tile(ms, 256)
    tq = _tile(t, 256)
    new_pad = 8

    rows_c = -(-(b + db) // 8) * 8
    c_all = jnp.pad(jnp.concatenate([c_prompt, c_sample], axis=0), ((0, rows_c - b - db), (0, 0)))
    mod = _adaln(c_all, w_mod, b_mod)

    cos_p, sin_p = _rope_tables(jnp.arange(t), MLA_HEADS)
    cos_s, sin_s = _rope_tables(past_len + jnp.arange(dt), MLA_HEADS)
    cos_s, sin_s = jnp.tile(cos_s, (db, 1)), jnp.tile(sin_s, (db, 1))
    exp_p = _expand_matrix(t // NSA_BLOCK, t)
    exp_s = _expand_matrix(past_len // NSA_BLOCK, past_len)
    tf = 512
    cache_mla_t = jnp.swapaxes(cache_mla, 2, 3)
    cache_nsa_t = jnp.transpose(cache_nsa, (0, 1, 3, 4, 2)).reshape(depth, -1, 4 * NSA_D, page)
    win_t = jnp.transpose(state_nsa_win, (0, 1, 3, 4, 2)).reshape(depth, db, 2 * NSA_D, -1)

    xp, xs = x_prompt.reshape(mp, d), x_sample.reshape(ms, d)
    outs = {k: [] for k in ("mla_p", "mla_s", "diff_p", "diff_s", "nsa_p", "nsa_s", "win_p", "win_s")}
    for i in range(depth):
        lam_init = 0.8 - 0.6 * math.exp(-0.3 * i)
        w_small, w_mg = _prep_w_in(w_in[i])
        w_uq_r = _prep_w_uq(mla_w_uq[i])
        w_uk_bd = _block_diag(jnp.transpose(mla_w_uk[i], (1, 2, 0))).astype(BF16)
        w_uv_bd = _block_diag(jnp.transpose(mla_w_uv[i], (1, 0, 2))).astype(BF16)
        w_r, pos_r = _prep_cmp(nsa_w_cmp[i], nsa_pos_cmp[i])
        w_branch16, w_out16 = w_branch[i].astype(BF16), w_out[i].astype(BF16)
        st_p = _Stream(mp, tm_p, mod[i, :b].reshape(b, 1, 6 * d), t // tm_p, d)
        st_s = _Stream(ms, tm_s, jnp.repeat(mod[i, b:b + db], dt, axis=0).reshape(ms // tm_s, tm_s, 6 * d), 1, d)

        new_x = []
        for x, st, is_prompt in ((xp, st_p, True), (xs, st_s, False)):
            tm = st.tm
            h = _norm_mod(x, g_attn[i], st, 1, 0)
            proj32, proj16, diff_rows, nsa_rows, win_rows = _in_proj(h, w_small, tm)
            gates16 = _matmul(h, w_mg, tm, 1024, mode="sigmoid")
            q_nope, q_rope, mla_rows, mla_rows16 = _mla_prep(
                proj32, mla_g_q[i], mla_g_kv[i], w_uq_r,
                cos_p if is_prompt else cos_s, sin_p if is_prompt else sin_s, tm)
            q_lat = _matmul(q_nope, w_uk_bd, tm, w_uk_bd.shape[1])
            if is_prompt:
                o_lat = _mla_prompt(q_lat, q_rope, mla_rows16, b, t, tq)
                o_diff = _diff_prompt(proj16, diff_lambda[i], diff_g_sub[i], lam_init, b, t, tq)
                win16 = proj16[:, C_WIN:C_WIN + 2 * NSA_D].reshape(b, t, 2 * NSA_D)
                win_pad16 = jnp.pad(win16, ((0, 0), (NSA_WINDOW, 0), (0, 0)))
                o_nsa = _nsa_prompt(proj32, proj16, win_pad16, pos_r, w_r, exp_p, b, t, tq)
                outs["mla_p"].append(mla_rows.reshape(b, t, -1))
                outs["diff_p"].append(diff_rows.reshape(b, t, 2, DIFF_D_V))
                outs["nsa_p"].append(nsa_rows.reshape(b, t, 4, NSA_D))
                win = win_rows.reshape(b, t, 2, NSA_D)
                outs["win_p"].append(win[:, t - min(NSA_WINDOW, t):])
            else:
                nh = MLA_HEADS
                q_mla = jnp.concatenate([q_lat.reshape(ms, nh, MLA_D_C), q_rope.reshape(ms, nh, MLA_D_ROPE)],
                                        axis=2).reshape(db, dt * nh, MLA_D_C + MLA_D_ROPE)
                new_mla = _pad_rows(mla_rows16.reshape(db, dt, -1), new_pad)
                o_lat = _decode_call(
                    functools.partial(_mla_decode_kernel, n_new=dt), "mla_decode", page_table, cache_mla_t, i,
                    [q_mla, new_mla], MLA_D_C,
                    [pltpu.VMEM((2, cache_mla_t.shape[2], past_len), F32)]).reshape(ms, nh * MLA_D_C)
                dq = proj16[:, C_DQ:C_DQ + DIFF_HEADS * 2 * DIFF_D].reshape(db, dt * DIFF_HEADS, 2, DIFF_D)
                zq = jnp.zeros_like(dq[:, :, 0])
                q_diff = jnp.concatenate([jnp.concatenate([dq[:, :, 0], zq], axis=2),
                                          jnp.concatenate([zq, dq[:, :, 1]], axis=2)], axis=1)
                new_diff = _pad_rows(proj16[:, C_DKV:C_DKV + 2 * DIFF_D_V].reshape(db, dt, -1), new_pad)
                o_diff = _decode_call(
                    functools.partial(_diff_decode_kernel, n_new=dt, lam_init=lam_init, past_len=past_len),
                    "diff_decode", page_table, cache_diff, i, [q_diff, new_diff], DIFF_D_V,
                    [pltpu.VMEM((2, 2, past_len, DIFF_D_V), F32)],
                    extra_inputs=[diff_lambda[i], diff_g_sub[i].reshape(1, -1)],
                    out_rows=dt * DIFF_HEADS).reshape(ms, DIFF_HEADS * DIFF_D_V)
                q_nsa = proj16[:, C_NQ:C_NQ + NSA_HEADS * NSA_D].reshape(db, dt * NSA_HEADS, NSA_D)
                new_nsa = _pad_rows(proj16[:, C_NSA:C_NSA + 6 * NSA_D].reshape(db, dt, -1), new_pad)
                ng = proj32[:, C_TAIL + TAIL_NG:C_TAIL + TAIL_NG + 3 * NSA_HEADS].reshape(db, dt * NSA_HEADS, 3)
                o_nsa = _decode_call(
                    functools.partial(_nsa_decode_kernel, n_new=dt, past_len=past_len),
                    "nsa_decode", page_table, cache_nsa_t, i, [q_nsa, new_nsa, ng], NSA_D,
                    [pltpu.VMEM((2, 4 * NSA_D, past_len), F32), pltpu.VMEM((past_len, 2 * NSA_D), F32)],
                    layer_inputs=[win_t], extra_inputs=[pos_r, w_r, exp_s]).reshape(ms, NSA_HEADS * NSA_D)
                outs["mla_s"].append(mla_rows.reshape(db, dt, -1))
                outs["diff_s"].append(diff_rows.reshape(db, dt, 2, DIFF_D_V))
                outs["nsa_s"].append(nsa_rows.reshape(db, dt, 4, NSA_D))
                win_all = jnp.concatenate([state_nsa_win[i], win_rows.reshape(db, dt, 2, NSA_D)], axis=1)
                outs["win_s"].append(win_all[:, win_all.shape[1] - min(NSA_WINDOW, win_all.shape[1]):])
            o_mla = _matmul(o_lat, w_uv_bd, tm, w_uv_bd.shape[1])
            merged = _branch_merge(o_mla, o_diff, o_nsa, gates16, w_branch16, tm)
            x = _matmul(merged, w_out16, tm, d, mode="residual", res=x, st=st, c_gate=2)
            j = i // 2
            if i % 2 == 0:
                hf = _norm_mod(x, g_ffn[i], st, 4, 3)
                x = _ffn(hf, w_gate_dense[j].astype(BF16)[None], w_up_dense[j].astype(BF16)[None],
                         w_down_dense[j].astype(BF16)[None], x, st, 5, tf)
            else:
                hf, eg = _norm_mod_route(x, g_ffn[i], st, 4, 3, w_router[j], b_router[j])
                fe = w_gate_exp.shape[3]
                fe_pad = -(-fe // tf) * tf
                wg = jnp.pad(w_gate_exp[j], ((0, 0), (0, 0), (0, fe_pad - fe))).astype(BF16)
                wu = jnp.pad(w_up_exp[j], ((0, 0), (0, 0), (0, fe_pad - fe))).astype(BF16)
                wd = jnp.pad(w_down_exp[j], ((0, 0), (0, fe_pad - fe), (0, 0))).astype(BF16)
                x = _ffn(hf, wg, wu, wd, x, st, 5, tf, expert_gates=eg)
            new_x.append(x)
        xp, xs = new_x

    y_p = _final_norm(xp, g_final, tm_p).reshape(b, t, d)
    y_s = _final_norm(xs, g_final, tm_s).reshape(db, dt, d)
    return (y_p, y_s,
            jnp.stack(outs["mla_p"]), jnp.stack(outs["mla_s"]),
            jnp.stack(outs["diff_p"]), jnp.stack(outs["diff_s"]),
            jnp.stack(outs["nsa_p"]), jnp.stack(outs["nsa_s"]),
            jnp.stack(outs["win_p"]), jnp.stack(outs["win_s"]))
```
